```python
import math
import jax, jax.numpy as jnp
from jax import lax
import numpy as np

D_MODEL = 1024
BATCH = 8
SEQ = 4096
DEPTH = 1
DEC_BATCH = 2
DEC_SEQ = 16384
PAST_LEN = 128

CONV_W = D_MODEL // 2
CONV_K = 31
M_HEADS = 4
M_W = D_MODEL // 2
M_DH = M_W // M_HEADS
CHUNK = 128
N_GATE = 4 * M_HEADS
N_IN = 2 * CONV_W + 4 * M_W + N_GATE
N_MEM = 256
X_HEADS = 4
X_DH = D_MODEL // X_HEADS
D_FF = 2816
FFN_K = 3
ALPHA = (2.0 * DEPTH) ** 0.25
BETA = (8.0 * DEPTH) ** -0.25
LN_EPS = 1e-5

kernel_name = "hybrid_conv_mlstm_encoder"


def layer_norm(x, g, b):
    xf = x.astype(jnp.float32)
    mu = jnp.mean(xf, axis=-1, keepdims=True)
    var = jnp.mean(jnp.square(xf - mu), axis=-1, keepdims=True)
    return ((xf - mu) * lax.rsqrt(var + LN_EPS) * g + b).astype(x.dtype)


def dwconv(x, w, b):
    pad = w.shape[0] // 2
    y = lax.conv_general_dilated(x, w[:, None, :].astype(x.dtype), window_strides=(1,),
                                 padding=[(pad, pad)], dimension_numbers=('NWC', 'WIO', 'NWC'),
                                 feature_group_count=x.shape[-1])
    return y + b


def mlstm_chunk_step(carry, inp):
    c_st, n_st, m_st = carry
    q, k, v, ig, lf = inp
    L = q.shape[2]
    lower = jnp.tril(jnp.ones((L, L), dtype=bool))
    b = jnp.cumsum(lf, axis=-1)
    a = b + m_st[..., None]
    d = jnp.where(lower, b[..., :, None] - b[..., None, :] + ig[..., None, :], -jnp.inf)
    m_t = jnp.maximum(a, jnp.max(d, axis=-1))
    dw = jnp.exp(d - m_t[..., None])
    aw = jnp.exp(a - m_t)
    s = jnp.einsum('bhtd,bhsd->bhts', q, k) * dw
    num = aw[..., None] * jnp.einsum('bhtd,bhde->bhte', q, c_st) + jnp.einsum('bhts,bhse->bhte', s, v)
    den = aw * jnp.einsum('bhtd,bhd->bht', q, n_st) + jnp.sum(s, axis=-1)
    h = num / jnp.maximum(jnp.abs(den), jnp.exp(-m_t))[..., None]
    m_new = m_t[..., -1]
    w_last = jnp.exp(b[..., -1:] - b + ig - m_new[..., None])
    decay = jnp.exp(b[..., -1] + m_st - m_new)
    c_new = decay[..., None, None] * c_st + jnp.einsum('bhs,bhsd,bhse->bhde', w_last, k, v)
    n_new = decay[..., None] * n_st + jnp.einsum('bhs,bhsd->bhd', w_last, k)
    return (c_new, n_new, m_new), h


def mlstm_direction(q, k, v, ig, lf):
    B, S, H, dh = q.shape
    nc = S // CHUNK

    def to_chunks(t):
        t = t.reshape((B, nc, CHUNK) + t.shape[2:])
        return jnp.moveaxis(jnp.moveaxis(t, 1, 0), 3, 2)

    init = (jnp.zeros((B, H, dh, dh), jnp.float32), jnp.zeros((B, H, dh), jnp.float32),
            jnp.zeros((B, H), jnp.float32))
    _, h = lax.scan(mlstm_chunk_step, init,
                    (to_chunks(q), to_chunks(k), to_chunks(v), to_chunks(ig), to_chunks(lf)))
    return jnp.swapaxes(jnp.moveaxis(h, 0, 1), 2, 3).reshape(B, S, H, dh)


def token_mixer(x, w_in, b_in, conv_w, conv_b, cln_g, cln_b, w_out, b_out):
    B, S, _ = x.shape
    p = x @ w_in + b_in
    u = p[..., :CONV_W] * jax.nn.sigmoid(p[..., CONV_W:2 * CONV_W])
    u = jax.nn.silu(layer_norm(dwconv(u, conv_w, conv_b), cln_g, cln_b))
    off = 2 * CONV_W
    f32 = jnp.float32
    q = p[..., off:off + M_W].astype(f32).reshape(B, S, M_HEADS, M_DH)
    k = p[..., off + M_W:off + 2 * M_W].astype(f32).reshape(B, S, M_HEADS, M_DH) * (M_DH ** -0.5)
    v = p[..., off + 2 * M_W:off + 3 * M_W].astype(f32).reshape(B, S, M_HEADS, M_DH)
    o = jax.nn.sigmoid(p[..., off + 3 * M_W:off + 4 * M_W])
    g = p[..., off + 4 * M_W:].astype(f32).reshape(B, S, 4, M_HEADS)
    ig_f, lf_f = g[:, :, 0], jax.nn.log_sigmoid(g[:, :, 1])
    ig_b, lf_b = g[:, :, 2], jax.nn.log_sigmoid(g[:, :, 3])
    h_f = mlstm_direction(q, k, v, ig_f, lf_f)
    fl = lambda t: jnp.flip(t, axis=1)
    h_b = fl(mlstm_direction(fl(q), fl(k), fl(v), fl(ig_b), fl(lf_b)))
    h = (h_f + h_b).reshape(B, S, M_W).astype(x.dtype) * o
    return jnp.concatenate([u, h], axis=-1) @ w_out + b_out


def cross_attention(x, mem, wq, wkv, wo):
    B, S, _ = x.shape
    M = mem.shape[1]
    q = (x @ wq).reshape(B, S, X_HEADS, X_DH)
    kv = mem @ wkv
    k = kv[..., :D_MODEL].reshape(B, M, X_HEADS, X_DH)
    v = kv[..., D_MODEL:].reshape(B, M, X_HEADS, X_DH)
    s = jnp.einsum('bshd,bmhd->bhsm', q, k).astype(jnp.float32) * (X_DH ** -0.5)
    pr = jax.nn.softmax(s, axis=-1).astype(x.dtype)
    return jnp.einsum('bhsm,bmhd->bshd', pr, v).reshape(B, S, D_MODEL) @ wo


def conv_ffn(x, w_up, b_up, cw, cb, w_down, b_down):
    p = x @ w_up + b_up
    gate = dwconv(p[..., :D_FF], cw, cb)
    return (jax.nn.gelu(gate) * p[..., D_FF:]) @ w_down + b_down


def encoder_layer(x, mem, w_in, b_in, conv_w, conv_b, cln_g, cln_b, w_out, b_out, ln1_g, ln1_b,
                  xq_w, xkv_w, xo_w, ln2_g, ln2_b, up_w, up_b, fconv_w, fconv_b, down_w, down_b,
                  ln3_g, ln3_b):
    x = layer_norm(ALPHA * x + token_mixer(x, w_in, b_in, conv_w, conv_b, cln_g, cln_b, w_out, b_out), ln1_g, ln1_b)
    x = layer_norm(ALPHA * x + cross_attention(x, mem, xq_w, xkv_w, xo_w), ln2_g, ln2_b)
    x = layer_norm(ALPHA * x + conv_ffn(x, up_w, up_b, fconv_w, fconv_b, down_w, down_b), ln3_g, ln3_b)
    return x


def setup_inputs(seed: int = 0) -> dict:
    key = jax.random.key(seed)
    ks = jax.random.split(key, 32)
    f32 = jnp.float32
    nrm = lambda k, shape, scale: jax.random.normal(k, shape, f32) * scale
    gain = lambda k, n: 1.0 + nrm(k, (DEPTH, n), 0.02)
    b_in = nrm(ks[5], (DEPTH, N_IN), 0.02)
    og = 2 * CONV_W + 4 * M_W
    fbias = jnp.linspace(3.0, 6.0, M_HEADS, dtype=f32)
    b_in = b_in.at[:, og + M_HEADS:og + 2 * M_HEADS].add(fbias)
    b_in = b_in.at[:, og + 3 * M_HEADS:og + 4 * M_HEADS].add(fbias)
    return {
        "x_prompt": nrm(ks[0], (BATCH, SEQ, D_MODEL), 1.0),
        "x_sample": nrm(ks[1], (DEC_BATCH, DEC_SEQ, D_MODEL), 1.0),
        "mem_prompt": nrm(ks[2], (BATCH, N_MEM, D_MODEL), 1.0),
        "mem_sample": nrm(ks[3], (DEC_BATCH, N_MEM, D_MODEL), 1.0),
        "w_in": nrm(ks[4], (DEPTH, D_MODEL, N_IN), D_MODEL ** -0.5),
        "b_in": b_in,
        "conv_w": nrm(ks[6], (DEPTH, CONV_K, CONV_W), CONV_K ** -0.5),
        "conv_b": nrm(ks[7], (DEPTH, CONV_W), 0.02),
        "cln_g": gain(ks[8], CONV_W),
        "cln_b": nrm(ks[9], (DEPTH, CONV_W), 0.02),
        "w_out": nrm(ks[10], (DEPTH, D_MODEL, D_MODEL), BETA * D_MODEL ** -0.5),
        "b_out": nrm(ks[11], (DEPTH, D_MODEL), 0.02),
        "ln1_g": gain(ks[12], D_MODEL),
        "ln1_b": nrm(ks[13], (DEPTH, D_MODEL), 0.02),
        "xq_w": nrm(ks[14], (DEPTH, D_MODEL, D_MODEL), D_MODEL ** -0.5),
        "xkv_w": jnp.concatenate([nrm(ks[15], (DEPTH, D_MODEL, D_MODEL), D_MODEL ** -0.5),
                                  nrm(ks[16], (DEPTH, D_MODEL, D_MODEL), BETA * D_MODEL ** -0.5)], axis=-1),
        "xo_w": nrm(ks[17], (DEPTH, D_MODEL, D_MODEL), BETA * D_MODEL ** -0.5),
        "ln2_g": gain(ks[18], D_MODEL),
        "ln2_b": nrm(ks[19], (DEPTH, D_MODEL), 0.02),
        "up_w": nrm(ks[20], (DEPTH, D_MODEL, 2 * D_FF), D_MODEL ** -0.5),
        "up_b": nrm(ks[21], (DEPTH, 2 * D_FF), 0.02),
        "fconv_w": nrm(ks[22], (DEPTH, FFN_K, D_FF), FFN_K ** -0.5),
        "fconv_b": nrm(ks[23], (DEPTH, D_FF), 0.02),
        "down_w": nrm(ks[24], (DEPTH, D_FF, D_MODEL), BETA * D_FF ** -0.5),
        "down_b": nrm(ks[25], (DEPTH, D_MODEL), 0.02),
        "ln3_g": gain(ks[26], D_MODEL),
        "ln3_b": nrm(ks[27], (DEPTH, D_MODEL), 0.02),
    }


def reference(x_prompt, x_sample, mem_prompt, mem_sample, w_in, b_in, conv_w, conv_b, cln_g, cln_b,
              w_out, b_out, ln1_g, ln1_b, xq_w, xkv_w, xo_w, ln2_g, ln2_b, up_w, up_b, fconv_w,
              fconv_b, down_w, down_b, ln3_g, ln3_b):
    y_prompt = x_prompt
    y_sample = x_sample
    for l in range(DEPTH):
        lp = (w_in[l], b_in[l], conv_w[l], conv_b[l], cln_g[l], cln_b[l], w_out[l], b_out[l],
              ln1_g[l], ln1_b[l], xq_w[l], xkv_w[l], xo_w[l], ln2_g[l], ln2_b[l], up_w[l], up_b[l],
              fconv_w[l], fconv_b[l], down_w[l], down_b[l], ln3_g[l], ln3_b[l])
        y_prompt = encoder_layer(y_prompt, mem_prompt, *lp)
        y_sample = encoder_layer(y_sample, mem_sample, *lp)
    return (y_prompt, y_sample)
```

```python
import functools

import jax
import jax.numpy as jnp
from jax import lax
from jax.experimental import pallas as pl
from jax.experimental.pallas import tpu as pltpu

D_MODEL = 1024
DEPTH = 1
CONV_W = D_MODEL // 2
CONV_K = 31
CONV_PAD = CONV_K // 2
M_HEADS = 4
M_W = D_MODEL // 2
M_DH = M_W // M_HEADS
CHUNK = 128
N_MAIN = 2 * CONV_W + 4 * M_W
N_MEM = 256
X_HEADS = 4
X_DH = D_MODEL // X_HEADS
D_FF = 2816
FFN_K = 3
ALPHA = (2.0 * DEPTH) ** 0.25
LN_EPS = 1e-5

F32 = jnp.float32
BF16 = jnp.bfloat16

VMEM_LIMIT_BYTES = 56 * 1024 * 1024
LANES = 128
HALO_ROWS = 16

IN_TM = 512
SCAN_CHUNKS = 2
MIX_TM = 256
FFN_TM = 512
FFN_SPLITS = 2


def _params(*sem):
    return pltpu.CompilerParams(dimension_semantics=sem, vmem_limit_bytes=VMEM_LIMIT_BYTES)


def _dot(a, b):
    return jnp.dot(a, b, preferred_element_type=F32)


def _dot_nt(a, b):
    return lax.dot_general(a, b, (((1,), (1,)), ((), ())), preferred_element_type=F32)


def _dot_tn(a, b):
    return lax.dot_general(a, b, (((0,), (0,)), ((), ())), preferred_element_type=F32)


def _layer_norm(x, g, b):
    mu = jnp.mean(x, axis=-1, keepdims=True)
    xc = x - mu
    var = jnp.mean(xc * xc, axis=-1, keepdims=True)
    return xc * lax.rsqrt(var + LN_EPS) * g + b


def _const_spec(shape):
    return pl.BlockSpec(shape, lambda *_: (0,) * len(shape))


def _kv_kernel(mem_ref, wkt_ref, wv_ref, kt_ref, v_ref):
    mem = mem_ref[0].astype(BF16)
    kt_ref[0] = _dot_nt(wkt_ref[...], mem).astype(BF16)
    v_ref[0] = _dot(mem, wv_ref[...]).astype(BF16)


def _kv_proj(mem, wkt, wv):
    nb = mem.shape[0]
    return pl.pallas_call(
        _kv_kernel,
        grid=(nb,),
        in_specs=[pl.BlockSpec((1, N_MEM, D_MODEL), lambda b: (b, 0, 0)),
                  _const_spec((D_MODEL, D_MODEL)), _const_spec((D_MODEL, D_MODEL))],
        out_specs=[pl.BlockSpec((1, D_MODEL, N_MEM), lambda b: (b, 0, 0)),
                   pl.BlockSpec((1, N_MEM, D_MODEL), lambda b: (b, 0, 0))],
        out_shape=[jax.ShapeDtypeStruct((nb, D_MODEL, N_MEM), BF16),
                   jax.ShapeDtypeStruct((nb, N_MEM, D_MODEL), BF16)],
        compiler_params=_params("parallel"),
        name="kv_proj",
    )(mem, wkt, wv)


def _split_hi_lo(x):
    hi = x.astype(BF16)
    lo = (x - hi.astype(F32)).astype(BF16)
    return hi, lo


def _in_proj_kernel(x_ref, w_ref, b_ref, wci_ref, bci_ref, wcf_ref, bcf_ref, wri_ref, bri_ref,
                    wrf_ref, brf_ref, u_ref, q_ref, k_ref, v_ref, o_ref, gic_ref, gbc_ref, grow_ref):
    xb = x_ref[...].astype(BF16)

    def seg(j):
        sl = slice(j * CONV_W, (j + 1) * CONV_W)
        return _dot(xb, w_ref[:, sl]) + b_ref[:, sl]

    u_ref[...] = (seg(0) * jax.nn.sigmoid(seg(1))).astype(BF16)
    q_ref[...] = seg(2).astype(BF16)
    k_ref[...] = (seg(3) * (M_DH ** -0.5)).astype(BF16)
    v_ref[...] = seg(4).astype(BF16)
    o_ref[...] = jax.nn.sigmoid(seg(5)).astype(BF16)

    gi_c = _dot(xb, wci_ref[...]) + bci_ref[...]
    lf_c = jax.nn.log_sigmoid(_dot(xb, wcf_ref[...]) + bcf_ref[...])
    gi_r = _dot_nt(wri_ref[...], xb) + bri_ref[...]
    lf_r = jax.nn.log_sigmoid(_dot_nt(wrf_ref[...], xb) + brf_ref[...])

    r = lax.broadcasted_iota(jnp.int32, (CHUNK, CHUNK), 0)
    c = lax.broadcasted_iota(jnp.int32, (CHUNK, CHUNK), 1)
    ge = jnp.where(r >= c, 1.0, 0.0).astype(BF16)
    le = jnp.where(r <= c, 1.0, 0.0).astype(BF16)
    fwd_lane = lax.broadcasted_iota(jnp.int32, (CHUNK, LANES), 1) < M_HEADS
    fwd_row = lax.broadcasted_iota(jnp.int32, (2 * 8, CHUNK), 0) < M_HEADS

    for ci in range(IN_TM // CHUNK):
        sl = slice(ci * CHUNK, (ci + 1) * CHUNK)
        hi, lo = _split_hi_lo(lf_c[sl])
        b_c = jnp.where(fwd_lane, _dot(ge, hi) + _dot(ge, lo), _dot(le, hi) + _dot(le, lo))
        gic_ref[sl, :] = gi_c[sl, :2 * M_HEADS]
        gbc_ref[sl, :] = b_c[:, :2 * M_HEADS]
        hi, lo = _split_hi_lo(lf_r[:, sl])
        b_r = jnp.where(fwd_row, _dot(hi, le) + _dot(lo, le), _dot(hi, ge) + _dot(lo, ge))
        grow_ref[:, sl] = b_r - gi_r[:, sl]


def _in_proj(x, w_main, b_main, gate_w):
    t = x.shape[0]
    tok = lambda n, dt: jax.ShapeDtypeStruct((t, n), dt)
    tok_spec = lambda n: pl.BlockSpec((IN_TM, n), lambda i: (i, 0))
    return pl.pallas_call(
        _in_proj_kernel,
        grid=(t // IN_TM,),
        in_specs=[tok_spec(D_MODEL), _const_spec(w_main.shape), _const_spec(b_main.shape)]
                 + [_const_spec(a.shape) for a in gate_w],
        out_specs=[tok_spec(CONV_W)] * 5 + [tok_spec(2 * M_HEADS)] * 2
                  + [pl.BlockSpec((2 * 8, IN_TM), lambda i: (0, i))],
        out_shape=[tok(CONV_W, BF16)] * 5 + [tok(2 * M_HEADS, F32)] * 2
                  + [jax.ShapeDtypeStruct((2 * 8, t), F32)],
        compiler_params=_params("parallel"),
        name="in_proj",
    )(x, w_main, b_main, *gate_w)


def _mlstm_chunk(q, k, v, ig_c, b_c, row_term, mask, b_end, cn_ref, m_ref, idx):
    m_prev = m_ref[idx][0:1, 0:1]
    cn_prev = cn_ref[idx]
    ones_col = jnp.where(lax.broadcasted_iota(jnp.int32, (CHUNK, M_DH), 1) == 0, 1.0, 0.0).astype(BF16)
    v_aug = jnp.concatenate([v, ones_col], axis=1)

    d = jnp.where(mask, b_c - row_term, -jnp.inf)
    m_loc = jnp.max(d, axis=1, keepdims=True)
    s = (_dot_nt(q, k) * jnp.exp(d - m_loc)).astype(BF16)
    intra = _dot(s, v_aug)
    inter = _dot(q, cn_prev.astype(BF16))
    a = b_c + m_prev
    m_t = jnp.maximum(a, m_loc)
    nd = jnp.exp(a - m_t) * inter + jnp.exp(m_loc - m_t) * intra
    den = jnp.maximum(jnp.abs(nd[:, M_DH:M_DH + 1]), jnp.exp(-m_t))
    h = nd[:, :M_DH] / den

    w_log = b_end - b_c + ig_c
    w_max = jnp.max(w_log, axis=0, keepdims=True)
    kw = (k.astype(F32) * jnp.exp(w_log - w_max)).astype(BF16)
    kv = _dot_tn(kw, v_aug)
    a_end = b_end + m_prev
    m_new = jnp.maximum(a_end, w_max)
    cn_ref[idx] = jnp.exp(a_end - m_new) * cn_prev + jnp.exp(w_max - m_new) * kv
    m_ref[idx] = jnp.broadcast_to(m_new, m_ref.shape[1:])
    return h


def _mlstm_kernel(qf_ref, kf_ref, vf_ref, gif_ref, gbf_ref, grf_ref,
                  qb_ref, kb_ref, vb_ref, gib_ref, gbb_ref, grb_ref,
                  hf_ref, hb_ref, cn_ref, m_ref):
    @pl.when(pl.program_id(1) == 0)
    def _():
        cn_ref[...] = jnp.zeros_like(cn_ref)
        m_ref[...] = jnp.zeros_like(m_ref)

    r = lax.broadcasted_iota(jnp.int32, (CHUNK, CHUNK), 0)
    c = lax.broadcasted_iota(jnp.int32, (CHUNK, CHUNK), 1)
    masks = (r >= c, r <= c)
    refs = ((qf_ref, kf_ref, vf_ref, gif_ref, gbf_ref, grf_ref, hf_ref),
            (qb_ref, kb_ref, vb_ref, gib_ref, gbb_ref, grb_ref, hb_ref))
    for step in range(SCAN_CHUNKS):
        for dirn in range(2):
            q_ref, k_ref, v_ref, gi_ref, gb_ref, gr_ref, h_ref = refs[dirn]
            ci = step if dirn == 0 else SCAN_CHUNKS - 1 - step
            sl = slice(ci * CHUNK, (ci + 1) * CHUNK)
            end = CHUNK - 1 if dirn == 0 else 0
            for hd in range(M_HEADS):
                col = dirn * M_HEADS + hd
                hs = slice(hd * M_DH, (hd + 1) * M_DH)
                b_c = gb_ref[sl, col:col + 1]
                h = _mlstm_chunk(q_ref[sl, hs], k_ref[sl, hs], v_ref[sl, hs],
                                 gi_ref[sl, col:col + 1], b_c, gr_ref[col:col + 1, sl],
                                 masks[dirn], b_c[end:end + 1, :], cn_ref, m_ref, col)
                h_ref[sl, hs] = h.astype(BF16)


def _mlstm_scan(q, k, v, gi_c, gb_c, g_row, nb, seq):
    t = q.shape[0]
    bt = SCAN_CHUNKS * CHUNK
    n = seq // bt
    fwd = lambda b, i: (b * n + i, 0)
    bwd = lambda b, i: (b * n + n - 1 - i, 0)
    fwd_r = lambda b, i: (0, b * n + i)
    bwd_r = lambda b, i: (0, b * n + n - 1 - i)

    def specs(tok_map, row_map):
        return [pl.BlockSpec((bt, M_W), tok_map)] * 3 + [pl.BlockSpec((bt, 2 * M_HEADS), tok_map)] * 2 \
               + [pl.BlockSpec((2 * 8, bt), row_map)]

    return pl.pallas_call(
        _mlstm_kernel,
        grid=(nb, n),
        in_specs=specs(fwd, fwd_r) + specs(bwd, bwd_r),
        out_specs=[pl.BlockSpec((bt, M_W), fwd), pl.BlockSpec((bt, M_W), bwd)],
        out_shape=[jax.ShapeDtypeStruct((t, M_W), BF16)] * 2,
        scratch_shapes=[pltpu.VMEM((2 * M_HEADS, M_DH, 2 * M_DH), F32),
                        pltpu.VMEM((2 * M_HEADS, 8, LANES), F32)],
        compiler_params=_params("parallel", "arbitrary"),
        name="mlstm_scan",
    )(q, k, v, gi_c, gb_c, g_row, q, k, v, gi_c, gb_c, g_row)


def _mix_attn_kernel(x_ref, u_ref, up_ref, un_ref, hf_ref, hb_ref, o_ref,
                     cw_ref, cb_ref, cg_ref, cbeta_ref, wo1_ref, bo1_ref, g1_ref, b1_ref,
                     wq_ref, kt_ref, v_ref, wo2_ref, g2_ref, b2_ref, out_ref, ucat_ref):
    i = pl.program_id(1)
    n = pl.num_programs(1)
    tm = MIX_TM
    ucat_ref[0:HALO_ROWS, :] = jnp.where(i > 0, up_ref[...].astype(F32), 0.0)
    ucat_ref[HALO_ROWS:HALO_ROWS + tm, :] = u_ref[...].astype(F32)
    ucat_ref[HALO_ROWS + tm:, :] = jnp.where(i < n - 1, un_ref[...].astype(F32), 0.0)
    acc = jnp.zeros((tm, CONV_W), F32) + cb_ref[...]
    for j in range(CONV_K):
        off = HALO_ROWS - CONV_PAD + j
        acc = acc + ucat_ref[off:off + tm, :] * cw_ref[j:j + 1, :]
    uc = jax.nn.silu(_layer_norm(acc, cg_ref[...], cbeta_ref[...]))

    hm = (hf_ref[...].astype(F32) + hb_ref[...].astype(F32)) * o_ref[...].astype(F32)
    mix = _dot(uc.astype(BF16), wo1_ref[0:CONV_W, :]) + _dot(hm.astype(BF16), wo1_ref[CONV_W:, :]) + bo1_ref[...]
    x1 = _layer_norm(ALPHA * x_ref[...] + mix, g1_ref[...], b1_ref[...])

    qx = _dot(x1.astype(BF16), wq_ref[...]).astype(BF16)
    heads = []
    for hd in range(X_HEADS):
        hs = slice(hd * X_DH, (hd + 1) * X_DH)
        sc = _dot(qx[:, hs], kt_ref[0, hs, :]) * (X_DH ** -0.5)
        sc = sc - jnp.max(sc, axis=-1, keepdims=True)
        e = jnp.exp(sc)
        pr = e / jnp.sum(e, axis=-1, keepdims=True)
        heads.append(_dot(pr.astype(BF16), v_ref[0, :, hs]).astype(BF16))
    att = _dot(jnp.concatenate(heads, axis=1), wo2_ref[...])
    out_ref[...] = _layer_norm(ALPHA * x1 + att, g2_ref[...], b2_ref[...])


def _mix_attn(x, u, hf, hb, o, kt, v, weights, nb, seq):
    t = x.shape[0]
    tm = MIX_TM
    n = seq // tm
    hpb = tm // HALO_ROWS
    nh = seq // HALO_ROWS
    tok = lambda b, i: (b * n + i, 0)
    prev = lambda b, i: (b * nh + jnp.maximum(i * hpb - 1, 0), 0)
    nxt = lambda b, i: (b * nh + jnp.minimum((i + 1) * hpb, nh - 1), 0)
    (cw, cb, cg, cbeta, wo1, bo1, g1, b1, wq, wo2, g2, b2) = weights
    cs = _const_spec
    return pl.pallas_call(
        _mix_attn_kernel,
        grid=(nb, n),
        in_specs=[pl.BlockSpec((tm, D_MODEL), tok), pl.BlockSpec((tm, CONV_W), tok),
                  pl.BlockSpec((HALO_ROWS, CONV_W), prev), pl.BlockSpec((HALO_ROWS, CONV_W), nxt),
                  pl.BlockSpec((tm, M_W), tok), pl.BlockSpec((tm, M_W), tok), pl.BlockSpec((tm, M_W), tok),
                  cs(cw.shape), cs(cb.shape), cs(cg.shape), cs(cbeta.shape), cs(wo1.shape), cs(bo1.shape),
                  cs(g1.shape), cs(b1.shape), cs(wq.shape),
                  pl.BlockSpec((1, D_MODEL, N_MEM), lambda b, i: (b, 0, 0)),
                  pl.BlockSpec((1, N_MEM, D_MODEL), lambda b, i: (b, 0, 0)),
                  cs(wo2.shape), cs(g2.shape), cs(b2.shape)],
        out_specs=pl.BlockSpec((tm, D_MODEL), tok),
        out_shape=jax.ShapeDtypeStruct((t, D_MODEL), F32),
        scratch_shapes=[pltpu.VMEM((tm + 2 * HALO_ROWS, CONV_W), F32)],
        compiler_params=_params("parallel", "parallel"),
        name="mix_attn",
    )(x, u, u, u, hf, hb, o, cw, cb, cg, cbeta, wo1, bo1, g1, b1, wq, kt, v, wo2, g2, b2)


def _ffn_halo_kernel(x_ref, w_ref, b_ref, out_ref):
    out_ref[...] = _dot(x_ref[...].astype(BF16), w_ref[...]) + b_ref[...]


def _ffn_halo(rows, w_gate, b_gate):
    r = rows.shape[0]
    return pl.pallas_call(
        _ffn_halo_kernel,
        grid=(1,),
        in_specs=[_const_spec(rows.shape), _const_spec(w_gate.shape), _const_spec(b_gate.shape)],
        out_specs=_const_spec((r, D_FF)),
        out_shape=jax.ShapeDtypeStruct((r, D_FF), F32),
        compiler_params=_params("arbitrary"),
        name="ffn_halo",
    )(rows, w_gate, b_gate)


def _conv_ffn_kernel(x_ref, halo_ref, wg_ref, bg_ref, wv_ref, bv_ref, cw_ref, cb_ref, wd_ref, bd_ref,
                     g3_ref, b3_ref, out_ref):
    i = pl.program_id(1)
    n = pl.num_programs(1)
    tm = FFN_TM
    x = x_ref[...]
    xb = x.astype(BF16)
    row = lax.broadcasted_iota(jnp.int32, (tm, 1), 0)
    halo = halo_ref[0]
    acc = jnp.zeros((tm, D_MODEL), F32) + bd_ref[...]
    w = D_FF // FFN_SPLITS
    for s in range(FFN_SPLITS):
        sl = slice(s * w, (s + 1) * w)
        gp = _dot(xb, wg_ref[:, sl]) + bg_ref[:, sl]
        before = jnp.where(i > 0, halo[0:1, sl], 0.0)
        after = jnp.where(i < n - 1, halo[1:2, sl], 0.0)
        g_prev = jnp.where(row == 0, before, pltpu.roll(gp, 1, 0))
        g_next = jnp.where(row == tm - 1, after, pltpu.roll(gp, tm - 1, 0))
        gate = g_prev * cw_ref[0:1, sl] + gp * cw_ref[1:2, sl] + g_next * cw_ref[2:3, sl] + cb_ref[:, sl]
        val = _dot(xb, wv_ref[:, sl]) + bv_ref[:, sl]
        act = (jax.nn.gelu(gate) * val).astype(BF16)
        acc = acc + _dot(act, wd_ref[sl, :])
    out_ref[...] = _layer_norm(ALPHA * x + acc, g3_ref[...], b3_ref[...])


def _conv_ffn(x2, weights, nb, seq):
    t = x2.shape[0]
    tm = FFN_TM
    n = seq // tm
    (wg, bg, wv, bv, cw, cb, wd, bd, g3, b3) = weights
    xr = x2.reshape(nb, n, tm, D_MODEL)
    last = xr[:, :, tm - 1]
    first = xr[:, :, 0]
    before = jnp.concatenate([last[:, :1], last[:, :-1]], axis=1)
    after = jnp.concatenate([first[:, 1:], first[:, -1:]], axis=1)
    rows = jnp.stack([before, after], axis=2).reshape(nb * n * 2, D_MODEL)
    halo = _ffn_halo(rows, wg, bg).reshape(nb * n, 2, D_FF)

    tok = lambda b, i: (b * n + i, 0)
    cs = _const_spec
    return pl.pallas_call(
        _conv_ffn_kernel,
        grid=(nb, n),
        in_specs=[pl.BlockSpec((tm, D_MODEL), tok),
                  pl.BlockSpec((1, 2, D_FF), lambda b, i: (b * n + i, 0, 0)),
                  cs(wg.shape), cs(bg.shape), cs(wv.shape), cs(bv.shape), cs(cw.shape), cs(cb.shape),
                  cs(wd.shape), cs(bd.shape), cs(g3.shape), cs(b3.shape)],
        out_specs=pl.BlockSpec((tm, D_MODEL), tok),
        out_shape=jax.ShapeDtypeStruct((t, D_MODEL), F32),
        compiler_params=_params("parallel", "parallel"),
        name="conv_ffn",
    )(x2, halo, wg, bg, wv, bv, cw, cb, wd, bd, g3, b3)


def _pad_cols(w, n):
    return jnp.pad(w, ((0, 0), (0, n - w.shape[1])))


def _pad_rows(w, n):
    return jnp.pad(w, ((0, n - w.shape[0]), (0, 0)))


def _prep_layer(w_in, b_in, conv_w, conv_b, cln_g, cln_b, w_out, b_out, ln1_g, ln1_b, xq_w, xkv_w, xo_w,
                ln2_g, ln2_b, up_w, up_b, fconv_w, fconv_b, down_w, down_b, ln3_g, ln3_b):
    row = lambda a: a.reshape(1, -1)
    wg = w_in[:, N_MAIN:].reshape(D_MODEL, 4, M_HEADS)
    bg = b_in[N_MAIN:].reshape(4, M_HEADS)
    wi = jnp.concatenate([wg[:, 0], wg[:, 2]], axis=1)
    wf = jnp.concatenate([wg[:, 1], wg[:, 3]], axis=1)
    bi = jnp.concatenate([bg[0], bg[2]])
    bf = jnp.concatenate([bg[1], bg[3]])
    gate_w = (_pad_cols(wi, LANES).astype(BF16), _pad_cols(row(bi), LANES),
              _pad_cols(wf, LANES).astype(BF16), _pad_cols(row(bf), LANES),
              _pad_rows(wi.T, 2 * 8).astype(BF16), _pad_rows(bi.reshape(-1, 1), 2 * 8),
              _pad_rows(wf.T, 2 * 8).astype(BF16), _pad_rows(bf.reshape(-1, 1), 2 * 8))
    in_w = (w_in[:, :N_MAIN].astype(BF16), row(b_in[:N_MAIN]), gate_w)
    kv_w = (xkv_w[:, :D_MODEL].T.astype(BF16), xkv_w[:, D_MODEL:].astype(BF16))
    mix_w = (conv_w, row(conv_b), row(cln_g), row(cln_b), w_out.astype(BF16), row(b_out), row(ln1_g), row(ln1_b),
             xq_w.astype(BF16), xo_w.astype(BF16), row(ln2_g), row(ln2_b))
    ffn_w = (up_w[:, :D_FF].astype(BF16), row(up_b[:D_FF]), up_w[:, D_FF:].astype(BF16), row(up_b[D_FF:]),
             fconv_w, row(fconv_b), down_w.astype(BF16), row(down_b), row(ln3_g), row(ln3_b))
    return in_w, kv_w, mix_w, ffn_w


def _encoder_layer(x, mem, prepped):
    in_w, kv_w, mix_w, ffn_w = prepped
    nb, seq, _ = x.shape
    xf = x.reshape(nb * seq, D_MODEL)
    kt, v_mem = _kv_proj(mem, *kv_w)
    u, q, k, v, o, gi_c, gb_c, g_row = _in_proj(xf, *in_w)
    hf, hb = _mlstm_scan(q, k, v, gi_c, gb_c, g_row, nb, seq)
    x2 = _mix_attn(xf, u, hf, hb, o, kt, v_mem, mix_w, nb, seq)
    x3 = _conv_ffn(x2, ffn_w, nb, seq)
    return x3.reshape(nb, seq, D_MODEL)


def kernel(x_prompt, x_sample, mem_prompt, mem_sample, w_in, b_in, conv_w, conv_b, cln_g, cln_b, w_out, b_out,
           ln1_g, ln1_b, xq_w, xkv_w, xo_w, ln2_g, ln2_b, up_w, up_b, fconv_w, fconv_b, down_w, down_b,
           ln3_g, ln3_b):
    layer_w = (w_in, b_in, conv_w, conv_b, cln_g, cln_b, w_out, b_out, ln1_g, ln1_b, xq_w, xkv_w, xo_w,
               ln2_g, ln2_b, up_w, up_b, fconv_w, fconv_b, down_w, down_b, ln3_g, ln3_b)
    y_prompt, y_sample = x_prompt, x_sample
    for l in range(DEPTH):
        prepped = _prep_layer(*[w[l] for w in layer_w])
        y_prompt = _encoder_layer(y_prompt, mem_prompt, prepped)
        y_sample = _encoder_layer(y_sample, mem_sample, prepped)
    return (y_prompt, y_sample)
```

```python
import functools

import jax
import jax.numpy as jnp
from jax import lax
from jax.experimental import pallas as pl
from jax.experimental.pallas import tpu as pltpu

D_MODEL = 1024
DEPTH = 1
CONV_W = D_MODEL // 2
CONV_K = 31
CONV_PAD = CONV_K // 2
M_HEADS = 4
M_W = D_MODEL // 2
M_DH = M_W // M_HEADS
CHUNK = 128
N_MAIN = 2 * CONV_W + 4 * M_W
N_MEM = 256
X_HEADS = 4
X_DH = D_MODEL // X_HEADS
D_FF = 2816
FFN_K = 3
ALPHA = (2.0 * DEPTH) ** 0.25
LN_EPS = 1e-5

F32 = jnp.float32
BF16 = jnp.bfloat16

VMEM_LIMIT_BYTES = 56 * 1024 * 1024
LANES = 128
SUBLANES = 8
HALO_ROWS = 16
CONV_ROWS = 32

IN_TM = 512
SCAN_CHUNKS = 4
MIX_TM = 256
FFN_TM = 512
FFN_SPLITS = 2


def _params(*sem):
    return pltpu.CompilerParams(dimension_semantics=sem, vmem_limit_bytes=VMEM_LIMIT_BYTES)


def _dot(a, b):
    return jnp.dot(a, b, preferred_element_type=F32)


def _dot_nt(a, b):
    return lax.dot_general(a, b, (((1,), (1,)), ((), ())), preferred_element_type=F32)


def _dot_tn(a, b):
    return lax.dot_general(a, b, (((0,), (0,)), ((), ())), preferred_element_type=F32)


def _layer_norm(x, g, b):
    mu = jnp.mean(x, axis=-1, keepdims=True)
    xc = x - mu
    var = jnp.mean(xc * xc, axis=-1, keepdims=True)
    return xc * lax.rsqrt(var + LN_EPS) * g + b


def _const_spec(shape):
    return pl.BlockSpec(shape, lambda *_: (0,) * len(shape))


def _kv_kernel(mem_ref, wkt_ref, wv_ref, kt_ref, v_ref):
    mem = mem_ref[0].astype(BF16)
    kt_ref[0] = _dot_nt(wkt_ref[...], mem).astype(BF16)
    v_ref[0] = _dot(mem, wv_ref[...]).astype(BF16)


def _kv_proj(mem, wkt, wv):
    nb = mem.shape[0]
    return pl.pallas_call(
        _kv_kernel,
        grid=(nb,),
        in_specs=[pl.BlockSpec((1, N_MEM, D_MODEL), lambda b: (b, 0, 0)),
                  _const_spec((D_MODEL, D_MODEL)), _const_spec((D_MODEL, D_MODEL))],
        out_specs=[pl.BlockSpec((1, D_MODEL, N_MEM), lambda b: (b, 0, 0)),
                   pl.BlockSpec((1, N_MEM, D_MODEL), lambda b: (b, 0, 0))],
        out_shape=[jax.ShapeDtypeStruct((nb, D_MODEL, N_MEM), BF16),
                   jax.ShapeDtypeStruct((nb, N_MEM, D_MODEL), BF16)],
        compiler_params=_params("parallel"),
        name="kv_proj",
    )(mem, wkt, wv)


def _split_hi_lo(x):
    hi = x.astype(BF16)
    lo = (x - hi.astype(F32)).astype(BF16)
    return hi, lo


def _in_proj_kernel(x_ref, w_ref, b_ref, wt_ref, bt_ref, wci_ref, bci_ref, wcf_ref, bcf_ref, wri_ref, bri_ref,
                    wrf_ref, brf_ref, u_ref, k_ref, o_ref, qt_ref, vt_ref, cc_ref, br_ref, cr_ref):
    xb = x_ref[...].astype(BF16)

    def seg(j):
        sl = slice(j * CONV_W, (j + 1) * CONV_W)
        return _dot(xb, w_ref[:, sl]) + b_ref[:, sl]

    u_ref[...] = (seg(0) * jax.nn.sigmoid(seg(1))).astype(BF16)
    k_ref[...] = (seg(2) * (M_DH ** -0.5)).astype(BF16)
    o_ref[...] = jax.nn.sigmoid(seg(3)).astype(BF16)
    qt_ref[...] = (_dot_nt(wt_ref[0:M_W, :], xb) + bt_ref[0:M_W, :]).astype(BF16)
    vt_ref[...] = (_dot_nt(wt_ref[M_W:, :], xb) + bt_ref[M_W:, :]).astype(BF16)

    gi_c = _dot(xb, wci_ref[...]) + bci_ref[...]
    lf_c = jax.nn.log_sigmoid(_dot(xb, wcf_ref[...]) + bcf_ref[...])
    gi_r = _dot_nt(wri_ref[...], xb) + bri_ref[...]
    lf_r = jax.nn.log_sigmoid(_dot_nt(wrf_ref[...], xb) + brf_ref[...])

    r = lax.broadcasted_iota(jnp.int32, (CHUNK, CHUNK), 0)
    c = lax.broadcasted_iota(jnp.int32, (CHUNK, CHUNK), 1)
    ge = jnp.where(r >= c, 1.0, 0.0).astype(BF16)
    le = jnp.where(r <= c, 1.0, 0.0).astype(BF16)
    fwd_lane = lax.broadcasted_iota(jnp.int32, (CHUNK, LANES), 1) < M_HEADS
    fwd_row = lax.broadcasted_iota(jnp.int32, (2 * 8, CHUNK), 0) < M_HEADS

    for ci in range(IN_TM // CHUNK):
        sl = slice(ci * CHUNK, (ci + 1) * CHUNK)
        hi, lo = _split_hi_lo(lf_c[sl])
        b_c = jnp.where(fwd_lane, _dot(ge, hi) + _dot(ge, lo), _dot(le, hi) + _dot(le, lo))
        cc_ref[sl, :] = (b_c - gi_c[sl])[:, :2 * M_HEADS]
        hi, lo = _split_hi_lo(lf_r[:, sl])
        b_r = jnp.where(fwd_row, _dot(hi, le) + _dot(lo, le), _dot(hi, ge) + _dot(lo, ge))
        br_ref[:, sl] = b_r
        cr_ref[:, sl] = b_r - gi_r[:, sl]


def _in_proj(x, w_tok, b_tok, w_fm, b_fm, gate_w):
    t = x.shape[0]
    tok_spec = lambda n: pl.BlockSpec((IN_TM, n), lambda i: (i, 0))
    fm_spec = lambda n: pl.BlockSpec((n, IN_TM), lambda i: (0, i))
    consts = (w_tok, b_tok, w_fm, b_fm) + tuple(gate_w)
    return pl.pallas_call(
        _in_proj_kernel,
        grid=(t // IN_TM,),
        in_specs=[tok_spec(D_MODEL)] + [_const_spec(a.shape) for a in consts],
        out_specs=[tok_spec(CONV_W)] * 3 + [fm_spec(M_W)] * 2 + [tok_spec(2 * M_HEADS)] + [fm_spec(2 * 8)] * 2,
        out_shape=[jax.ShapeDtypeStruct((t, CONV_W), BF16)] * 3 + [jax.ShapeDtypeStruct((M_W, t), BF16)] * 2
                  + [jax.ShapeDtypeStruct((t, 2 * M_HEADS), F32)] + [jax.ShapeDtypeStruct((2 * 8, t), F32)] * 2,
        compiler_params=_params("parallel"),
        name="in_proj",
    )(x, *consts)


def _mlstm_chunk(k, qt, vt_aug, c_col, b_row, c_row, mask, b_end, cn_ref, m_ref, idx):
    m_prev = m_ref[idx][0:1, 0:1]
    cn_prev = cn_ref[idx]

    d = jnp.where(mask, b_row - c_col, -jnp.inf)
    m_loc = jnp.max(d, axis=0, keepdims=True)
    s = (_dot(k, qt) * jnp.exp(d - m_loc)).astype(BF16)
    intra = _dot(vt_aug, s)
    inter = _dot(cn_prev.astype(BF16), qt)
    a = b_row + m_prev
    m_t = jnp.maximum(a, m_loc)
    nd = jnp.exp(a - m_t) * inter + jnp.exp(m_loc - m_t) * intra
    den = jnp.maximum(jnp.abs(nd[M_DH:M_DH + 1, :]), jnp.exp(-m_t))
    ht = nd[:M_DH, :] / den

    w_log = b_end - c_row
    w_max = jnp.max(w_log, axis=1, keepdims=True)
    vw = (vt_aug.astype(F32) * jnp.exp(w_log - w_max)).astype(BF16)
    kv = _dot(vw, k)
    a_end = b_end + m_prev
    m_new = jnp.maximum(a_end, w_max)
    cn_ref[idx] = jnp.exp(a_end - m_new) * cn_prev + jnp.exp(w_max - m_new) * kv
    m_ref[idx] = jnp.broadcast_to(m_new, m_ref.shape[1:])
    return ht


def _mlstm_kernel(kf_ref, qtf_ref, vtf_ref, ccf_ref, brf_ref, crf_ref,
                  kb_ref, qtb_ref, vtb_ref, ccb_ref, brb_ref, crb_ref,
                  hf_ref, hb_ref, cn_ref, m_ref):
    @pl.when(pl.program_id(1) == 0)
    def _():
        cn_ref[...] = jnp.zeros_like(cn_ref)
        m_ref[...] = jnp.zeros_like(m_ref)

    r = lax.broadcasted_iota(jnp.int32, (CHUNK, CHUNK), 0)
    c = lax.broadcasted_iota(jnp.int32, (CHUNK, CHUNK), 1)
    masks = (r <= c, r >= c)
    ones_rows = jnp.where(lax.broadcasted_iota(jnp.int32, (M_DH, CHUNK), 0) == 0, 1.0, 0.0).astype(BF16)
    refs = ((kf_ref, qtf_ref, vtf_ref, ccf_ref, brf_ref, crf_ref, hf_ref),
            (kb_ref, qtb_ref, vtb_ref, ccb_ref, brb_ref, crb_ref, hb_ref))
    for step in range(SCAN_CHUNKS):
        for dirn in range(2):
            k_ref, qt_ref, vt_ref, cc_ref, br_ref, cr_ref, h_ref = refs[dirn]
            ci = step if dirn == 0 else SCAN_CHUNKS - 1 - step
            sl = slice(ci * CHUNK, (ci + 1) * CHUNK)
            end = CHUNK - 1 if dirn == 0 else 0
            for hd in range(M_HEADS):
                col = dirn * M_HEADS + hd
                hs = slice(hd * M_DH, (hd + 1) * M_DH)
                b_row = br_ref[col:col + 1, sl]
                vt_aug = jnp.concatenate([vt_ref[hs, sl], ones_rows], axis=0)
                ht = _mlstm_chunk(k_ref[sl, hs], qt_ref[hs, sl], vt_aug, cc_ref[sl, col:col + 1], b_row,
                                  cr_ref[col:col + 1, sl], masks[dirn], b_row[:, end:end + 1], cn_ref, m_ref, col)
                h_ref[sl, hs] = ht.T.astype(BF16)


def _mlstm_scan(k, qt, vt, c_col, b_row, c_row, nb, seq):
    t = k.shape[0]
    bt = SCAN_CHUNKS * CHUNK
    n = seq // bt
    fwd = lambda b, i: (b * n + i, 0)
    bwd = lambda b, i: (b * n + n - 1 - i, 0)
    fwd_r = lambda b, i: (0, b * n + i)
    bwd_r = lambda b, i: (0, b * n + n - 1 - i)

    def specs(tok_map, fm_map):
        return [pl.BlockSpec((bt, M_W), tok_map)] + [pl.BlockSpec((M_W, bt), fm_map)] * 2 \
               + [pl.BlockSpec((bt, 2 * M_HEADS), tok_map)] + [pl.BlockSpec((2 * 8, bt), fm_map)] * 2

    args = (k, qt, vt, c_col, b_row, c_row)
    return pl.pallas_call(
        _mlstm_kernel,
        grid=(nb, n),
        in_specs=specs(fwd, fwd_r) + specs(bwd, bwd_r),
        out_specs=[pl.BlockSpec((bt, M_W), fwd), pl.BlockSpec((bt, M_W), bwd)],
        out_shape=[jax.ShapeDtypeStruct((t, M_W), BF16)] * 2,
        scratch_shapes=[pltpu.VMEM((2 * M_HEADS, 2 * M_DH, M_DH), F32),
                        pltpu.VMEM((2 * M_HEADS, 8, LANES), F32)],
        compiler_params=_params("parallel", "arbitrary"),
        name="mlstm_scan",
    )(*args, *args)


def _mix_attn_kernel(x_ref, u_ref, up_ref, un_ref, hf_ref, hb_ref, o_ref,
                     cw_ref, cb_ref, cg_ref, cbeta_ref, wo1_ref, bo1_ref, g1_ref, b1_ref,
                     wq_ref, kt_ref, v_ref, wo2_ref, g2_ref, b2_ref, out_ref, ucat_ref, ush_ref, uc_ref):
    i = pl.program_id(1)
    n = pl.num_programs(1)
    tm = MIX_TM
    ucat_ref[0:HALO_ROWS, :] = jnp.where(i > 0, up_ref[...].astype(F32), 0.0)
    ucat_ref[HALO_ROWS:HALO_ROWS + tm, :] = u_ref[...].astype(F32)
    ucat_ref[HALO_ROWS + tm:, :] = jnp.where(i < n - 1, un_ref[...].astype(F32), 0.0)
    for r in range(1, SUBLANES):
        ush_ref[r - 1] = ucat_ref[r:r + ush_ref.shape[1], :]
    for rt in range(tm // CONV_ROWS):
        r0 = rt * CONV_ROWS
        acc = jnp.zeros((CONV_ROWS, CONV_W), F32) + cb_ref[...]
        for j in range(CONV_K):
            a, r = divmod(HALO_ROWS - CONV_PAD + j, SUBLANES)
            src = ucat_ref if r == 0 else ush_ref.at[r - 1]
            lo = r0 + a * SUBLANES
            acc = acc + src[lo:lo + CONV_ROWS, :] * cw_ref[j:j + 1, :]
        uc = jax.nn.silu(_layer_norm(acc, cg_ref[...], cbeta_ref[...]))
        uc_ref[r0:r0 + CONV_ROWS, :] = uc.astype(BF16)

    hm = (hf_ref[...].astype(F32) + hb_ref[...].astype(F32)) * o_ref[...].astype(F32)
    mix = _dot(uc_ref[...], wo1_ref[0:CONV_W, :]) + _dot(hm.astype(BF16), wo1_ref[CONV_W:, :]) + bo1_ref[...]
    x1 = _layer_norm(ALPHA * x_ref[...] + mix, g1_ref[...], b1_ref[...])

    qx = _dot(x1.astype(BF16), wq_ref[...]).astype(BF16)
    heads = []
    for hd in range(X_HEADS):
        hs = slice(hd * X_DH, (hd + 1) * X_DH)
        sc = _dot(qx[:, hs], kt_ref[0, hs, :]) * (X_DH ** -0.5)
        sc = sc - jnp.max(sc, axis=-1, keepdims=True)
        e = jnp.exp(sc)
        pr = e / jnp.sum(e, axis=-1, keepdims=True)
        heads.append(_dot(pr.astype(BF16), v_ref[0, :, hs]).astype(BF16))
    att = _dot(jnp.concatenate(heads, axis=1), wo2_ref[...])
    out_ref[...] = _layer_norm(ALPHA * x1 + att, g2_ref[...], b2_ref[...])


def _mix_attn(x, u, hf, hb, o, kt, v, weights, nb, seq):
    t = x.shape[0]
    tm = MIX_TM
    n = seq // tm
    hpb = tm // HALO_ROWS
    nh = seq // HALO_ROWS
    tok = lambda b, i: (b * n + i, 0)
    prev = lambda b, i: (b * nh + jnp.maximum(i * hpb - 1, 0), 0)
    nxt = lambda b, i: (b * nh + jnp.minimum((i + 1) * hpb, nh - 1), 0)
    (cw, cb, cg, cbeta, wo1, bo1, g1, b1, wq, wo2, g2, b2) = weights
    cs = _const_spec
    return pl.pallas_call(
        _mix_attn_kernel,
        grid=(nb, n),
        in_specs=[pl.BlockSpec((tm, D_MODEL), tok), pl.BlockSpec((tm, CONV_W), tok),
                  pl.BlockSpec((HALO_ROWS, CONV_W), prev), pl.BlockSpec((HALO_ROWS, CONV_W), nxt),
                  pl.BlockSpec((tm, M_W), tok), pl.BlockSpec((tm, M_W), tok), pl.BlockSpec((tm, M_W), tok),
                  cs(cw.shape), cs(cb.shape), cs(cg.shape), cs(cbeta.shape), cs(wo1.shape), cs(bo1.shape),
                  cs(g1.shape), cs(b1.shape), cs(wq.shape),
                  pl.BlockSpec((1, D_MODEL, N_MEM), lambda b, i: (b, 0, 0)),
                  pl.BlockSpec((1, N_MEM, D_MODEL), lambda b, i: (b, 0, 0)),
                  cs(wo2.shape), cs(g2.shape), cs(b2.shape)],
        out_specs=pl.BlockSpec((tm, D_MODEL), tok),
        out_shape=jax.ShapeDtypeStruct((t, D_MODEL), F32),
        scratch_shapes=[pltpu.VMEM((tm + 2 * HALO_ROWS, CONV_W), F32),
                        pltpu.VMEM((SUBLANES - 1, tm + 2 * HALO_ROWS - SUBLANES, CONV_W), F32),
                        pltpu.VMEM((tm, CONV_W), BF16)],
        compiler_params=_params("parallel", "parallel"),
        name="mix_attn",
    )(x, u, u, u, hf, hb, o, cw, cb, cg, cbeta, wo1, bo1, g1, b1, wq, kt, v, wo2, g2, b2)


def _ffn_halo_kernel(x_ref, w_ref, b_ref, out_ref):
    out_ref[...] = _dot(x_ref[...].astype(BF16), w_ref[...]) + b_ref[...]


def _ffn_halo(rows, w_gate, b_gate):
    r = rows.shape[0]
    return pl.pallas_call(
        _ffn_halo_kernel,
        grid=(1,),
        in_specs=[_const_spec(rows.shape), _const_spec(w_gate.shape), _const_spec(b_gate.shape)],
        out_specs=_const_spec((r, D_FF)),
        out_shape=jax.ShapeDtypeStruct((r, D_FF), F32),
        compiler_params=_params("arbitrary"),
        name="ffn_halo",
    )(rows, w_gate, b_gate)


def _conv_ffn_kernel(x_ref, halo_ref, wg_ref, bg_ref, wv_ref, bv_ref, cw_ref, cb_ref, wd_ref, bd_ref,
                     g3_ref, b3_ref, out_ref):
    i = pl.program_id(1)
    n = pl.num_programs(1)
    tm = FFN_TM
    x = x_ref[...]
    xb = x.astype(BF16)
    row = lax.broadcasted_iota(jnp.int32, (tm, 1), 0)
    halo = halo_ref[0]
    acc = jnp.zeros((tm, D_MODEL), F32) + bd_ref[...]
    w = D_FF // FFN_SPLITS
    for s in range(FFN_SPLITS):
        sl = slice(s * w, (s + 1) * w)
        gp = _dot(xb, wg_ref[:, sl]) + bg_ref[:, sl]
        before = jnp.where(i > 0, halo[0:1, sl], 0.0)
        after = jnp.where(i < n - 1, halo[1:2, sl], 0.0)
        g_prev = jnp.where(row == 0, before, pltpu.roll(gp, 1, 0))
        g_next = jnp.where(row == tm - 1, after, pltpu.roll(gp, tm - 1, 0))
        gate = g_prev * cw_ref[0:1, sl] + gp * cw_ref[1:2, sl] + g_next * cw_ref[2:3, sl] + cb_ref[:, sl]
        val = _dot(xb, wv_ref[:, sl]) + bv_ref[:, sl]
        act = (jax.nn.gelu(gate) * val).astype(BF16)
        acc = acc + _dot(act, wd_ref[sl, :])
    out_ref[...] = _layer_norm(ALPHA * x + acc, g3_ref[...], b3_ref[...])


def _conv_ffn(x2, weights, nb, seq):
    t = x2.shape[0]
    tm = FFN_TM
    n = seq // tm
    (wg, bg, wv, bv, cw, cb, wd, bd, g3, b3) = weights
    xr = x2.reshape(nb, n, tm, D_MODEL)
    last = xr[:, :, tm - 1]
    first = xr[:, :, 0]
    before = jnp.concatenate([last[:, :1], last[:, :-1]], axis=1)
    after = jnp.concatenate([first[:, 1:], first[:, -1:]], axis=1)
    rows = jnp.stack([before, after], axis=2).reshape(nb * n * 2, D_MODEL)
    halo = _ffn_halo(rows, wg, bg).reshape(nb * n, 2, D_FF)

    tok = lambda b, i: (b * n + i, 0)
    cs = _const_spec
    return pl.pallas_call(
        _conv_ffn_kernel,
        grid=(nb, n),
        in_specs=[pl.BlockSpec((tm, D_MODEL), tok),
                  pl.BlockSpec((1, 2, D_FF), lambda b, i: (b * n + i, 0, 0)),
                  cs(wg.shape), cs(bg.shape), cs(wv.shape), cs(bv.shape), cs(cw.shape), cs(cb.shape),
                  cs(wd.shape), cs(bd.shape), cs(g3.shape), cs(b3.shape)],
        out_specs=pl.BlockSpec((tm, D_MODEL), tok),
        out_shape=jax.ShapeDtypeStruct((t, D_MODEL), F32),
        compiler_params=_params("parallel", "parallel"),
        name="conv_ffn",
    )(x2, halo, wg, bg, wv, bv, cw, cb, wd, bd, g3, b3)


def _pad_cols(w, n):
    return jnp.pad(w, ((0, 0), (0, n - w.shape[1])))


def _pad_rows(w, n):
    return jnp.pad(w, ((0, n - w.shape[0]), (0, 0)))


def _prep_layer(w_in, b_in, conv_w, conv_b, cln_g, cln_b, w_out, b_out, ln1_g, ln1_b, xq_w, xkv_w, xo_w,
                ln2_g, ln2_b, up_w, up_b, fconv_w, fconv_b, down_w, down_b, ln3_g, ln3_b):
    row = lambda a: a.reshape(1, -1)
    wg = w_in[:, N_MAIN:].reshape(D_MODEL, 4, M_HEADS)
    bg = b_in[N_MAIN:].reshape(4, M_HEADS)
    wi = jnp.concatenate([wg[:, 0], wg[:, 2]], axis=1)
    wf = jnp.concatenate([wg[:, 1], wg[:, 3]], axis=1)
    bi = jnp.concatenate([bg[0], bg[2]])
    bf = jnp.concatenate([bg[1], bg[3]])
    gate_w = (_pad_cols(wi, LANES).astype(BF16), _pad_cols(row(bi), LANES),
              _pad_cols(wf, LANES).astype(BF16), _pad_cols(row(bf), LANES),
              _pad_rows(wi.T, 2 * 8).astype(BF16), _pad_rows(bi.reshape(-1, 1), 2 * 8),
              _pad_rows(wf.T, 2 * 8).astype(BF16), _pad_rows(bf.reshape(-1, 1), 2 * 8))
    seg = lambda a, j: a[..., j * CONV_W:(j + 1) * CONV_W]
    tok_segs, fm_segs = (0, 1, 3, 5), (2, 4)
    in_w = (jnp.concatenate([seg(w_in, j) for j in tok_segs], axis=1).astype(BF16),
            row(jnp.concatenate([seg(b_in, j) for j in tok_segs])),
            jnp.concatenate([seg(w_in, j).T for j in fm_segs], axis=0).astype(BF16),
            jnp.concatenate([seg(b_in, j) for j in fm_segs]).reshape(-1, 1),
            gate_w)
    kv_w = (xkv_w[:, :D_MODEL].T.astype(BF16), xkv_w[:, D_MODEL:].astype(BF16))
    mix_w = (conv_w, row(conv_b), row(cln_g), row(cln_b), w_out.astype(BF16), row(b_out), row(ln1_g), row(ln1_b),
             xq_w.astype(BF16), xo_w.astype(BF16), row(ln2_g), row(ln2_b))
    ffn_w = (up_w[:, :D_FF].astype(BF16), row(up_b[:D_FF]), up_w[:, D_FF:].astype(BF16), row(up_b[D_FF:]),
             fconv_w, row(fconv_b), down_w.astype(BF16), row(down_b), row(ln3_g), row(ln3_b))
    return in_w, kv_w, mix_w, ffn_w


def _encoder_layer(x, mem, prepped):
    in_w, kv_w, mix_w, ffn_w = prepped
    nb, seq, _ = x.shape
    xf = x.reshape(nb * seq, D_MODEL)
    kt, v_mem = _kv_proj(mem, *kv_w)
    u, k, o, qt, vt, c_col, b_row, c_row = _in_proj(xf, *in_w)
    hf, hb = _mlstm_scan(k, qt, vt, c_col, b_row, c_row, nb, seq)
    x2 = _mix_attn(xf, u, hf, hb, o, kt, v_mem, mix_w, nb, seq)
    x3 = _conv_ffn(x2, ffn_w, nb, seq)
    return x3.reshape(nb, seq, D_MODEL)


def kernel(x_prompt, x_sample, mem_prompt, mem_sample, w_in, b_in, conv_w, conv_b, cln_g, cln_b, w_out, b_out,
           ln1_g, ln1_b, xq_w, xkv_w, xo_w, ln2_g, ln2_b, up_w, up_b, fconv_w, fconv_b, down_w, down_b,
           ln3_g, ln3_b):
    layer_w = (w_in, b_in, conv_w, conv_b, cln_g, cln_b, w_out, b_out, ln1_g, ln1_b, xq_w, xkv_w, xo_w,
               ln2_g, ln2_b, up_w, up_b, fconv_w, fconv_b, down_w, down_b, ln3_g, ln3_b)
    y_prompt, y_sample = x_prompt, x_sample
    for l in range(DEPTH):
        prepped = _prep_layer(*[w[l] for w in layer_w])
        y_prompt = _encoder_layer(y_prompt, mem_prompt, prepped)
        y_sample = _encoder_layer(y_sample, mem_sample, prepped)
    return (y_prompt, y_sample)
```

```python
import functools

import jax
import jax.numpy as jnp
from jax import lax
from jax.experimental import pallas as pl
from jax.experimental.pallas import tpu as pltpu

D_MODEL = 1024
DEPTH = 1
CONV_W = D_MODEL // 2
CONV_K = 31
CONV_PAD = CONV_K // 2
M_HEADS = 4
M_W = D_MODEL // 2
M_DH = M_W // M_HEADS
CHUNK = 128
N_MAIN = 2 * CONV_W + 4 * M_W
N_MEM = 256
X_HEADS = 4
X_DH = D_MODEL // X_HEADS
D_FF = 2816
FFN_K = 3
ALPHA = (2.0 * DEPTH) ** 0.25
LN_EPS = 1e-5

F32 = jnp.float32
BF16 = jnp.bfloat16

VMEM_LIMIT_BYTES = 56 * 1024 * 1024
LANES = 128
SUBLANES = 8
HALO_ROWS = 16
CONV_ROWS = 32
CONV_DELAY = 2

IN_TM = 512
SCAN_CHUNKS = 4
MIX_TM = 256
FFN_TM = 512
FFN_SPLITS = 2


def _params(*sem):
    return pltpu.CompilerParams(dimension_semantics=sem, vmem_limit_bytes=VMEM_LIMIT_BYTES)


def _dot(a, b):
    return jnp.dot(a, b, preferred_element_type=F32)


def _dot_nt(a, b):
    return lax.dot_general(a, b, (((1,), (1,)), ((), ())), preferred_element_type=F32)


def _layer_norm(x, g, b):
    mu = jnp.mean(x, axis=-1, keepdims=True)
    xc = x - mu
    var = jnp.mean(xc * xc, axis=-1, keepdims=True)
    return xc * lax.rsqrt(var + LN_EPS) * g + b


def _const_spec(shape):
    return pl.BlockSpec(shape, lambda *_: (0,) * len(shape))


def _kv_kernel(mem_ref, wkt_ref, wv_ref, kt_ref, v_ref):
    mem = mem_ref[0].astype(BF16)
    kt_ref[0] = _dot_nt(wkt_ref[...], mem).astype(BF16)
    v_ref[0] = _dot(mem, wv_ref[...]).astype(BF16)


def _kv_proj(mem, wkt, wv):
    nb = mem.shape[0]
    return pl.pallas_call(
        _kv_kernel,
        grid=(nb,),
        in_specs=[pl.BlockSpec((1, N_MEM, D_MODEL), lambda b: (b, 0, 0)),
                  _const_spec((D_MODEL, D_MODEL)), _const_spec((D_MODEL, D_MODEL))],
        out_specs=[pl.BlockSpec((1, D_MODEL, N_MEM), lambda b: (b, 0, 0)),
                   pl.BlockSpec((1, N_MEM, D_MODEL), lambda b: (b, 0, 0))],
        out_shape=[jax.ShapeDtypeStruct((nb, D_MODEL, N_MEM), BF16),
                   jax.ShapeDtypeStruct((nb, N_MEM, D_MODEL), BF16)],
        compiler_params=_params("parallel"),
        name="kv_proj",
    )(mem, wkt, wv)


def _split_hi_lo(x):
    hi = x.astype(BF16)
    lo = (x - hi.astype(F32)).astype(BF16)
    return hi, lo


def _conv31_tasks(ucat_ref, ush_ref, cw_ref, cb_ref, cg_ref, cbeta_ref, out_ref, tm):
    n_sh = ush_ref.shape[1]

    def shifted_copies():
        ucat = ucat_ref[...]
        for r in range(1, SUBLANES):
            ush_ref[r - 1] = pltpu.roll(ucat, ucat.shape[0] - r, 0)[:n_sh]

    groups = CONV_ROWS // SUBLANES

    def rows(r0):
        acc = jnp.zeros((groups, SUBLANES, CONV_W), F32) + cb_ref[...]
        for j in range(CONV_K):
            a, r = divmod(HALO_ROWS - CONV_PAD + j, SUBLANES)
            src = ucat_ref if r == 0 else ush_ref.at[r - 1]
            lo = r0 + a * SUBLANES
            acc = acc + src[lo:lo + CONV_ROWS, :].reshape(groups, SUBLANES, CONV_W) * cw_ref[j]
        uc = jax.nn.silu(_layer_norm(acc.reshape(CONV_ROWS, CONV_W), cg_ref[...], cbeta_ref[...]))
        out_ref[r0:r0 + CONV_ROWS, :] = uc.astype(BF16)

    return [shifted_copies] + [functools.partial(rows, r0) for r0 in range(0, tm, CONV_ROWS)]


def _interleave(main, filler):
    done = 0
    for n, task in enumerate(main):
        task()
        upto = (n + 1) * len(filler) // len(main)
        for f in filler[done:upto]:
            f()
        done = upto


def _in_proj_kernel(x_ref, w_ref, b_ref, wt_ref, bt_ref, wg_ref, bg_ref, cw_ref, cb_ref, cg_ref, cbeta_ref,
                    uc_ref, k_ref, o_ref, qt_ref, vt_ref, cc_ref, br_ref, cr_ref, ucat_ref, ush_ref,
                    *, blocks_per_seq):
    i = pl.program_id(0)
    n_blocks = pl.num_programs(0) - CONV_DELAY
    tm = IN_TM
    slot_new = lax.rem(i, CONV_DELAY + 1)
    slot_prev = lax.rem(i + CONV_DELAY, CONV_DELAY + 1)
    slot_conv = lax.rem(i + 1, CONV_DELAY + 1)

    @pl.when(i == 0)
    def _():
        ucat_ref[...] = jnp.zeros_like(ucat_ref)

    xb = x_ref[...].astype(BF16)
    starts_seq = lax.rem(i, blocks_per_seq) == 0

    def seg(j):
        sl = slice(j * CONV_W, (j + 1) * CONV_W)
        return _dot(xb, w_ref[:, sl]) + b_ref[:, sl]

    def glu():
        u = seg(0) * jax.nn.sigmoid(seg(1))
        ucat_ref[slot_new, HALO_ROWS:HALO_ROWS + tm, :] = u
        ucat_ref[slot_new, 0:HALO_ROWS, :] = jnp.where(starts_seq, 0.0, ucat_ref[slot_prev, tm:tm + HALO_ROWS, :])
        ucat_ref[slot_prev, HALO_ROWS + tm:, :] = jnp.where(jnp.logical_or(starts_seq, i >= n_blocks), 0.0,
                                                            u[0:HALO_ROWS, :])

    def k_proj():
        k_ref[...] = (seg(2) * (M_DH ** -0.5)).astype(BF16)

    def o_proj():
        o_ref[...] = jax.nn.sigmoid(seg(3)).astype(BF16)

    def q_proj():
        qt_ref[...] = (_dot_nt(wt_ref[0:M_W, :], xb) + bt_ref[0:M_W, :]).astype(BF16)

    def v_proj():
        vt_ref[...] = (_dot_nt(wt_ref[M_W:, :], xb) + bt_ref[M_W:, :]).astype(BF16)

    def gates():
        g = _dot(xb, wg_ref[...]) + bg_ref[...]
        lf_c = jax.nn.log_sigmoid(g)
        gi_c = pltpu.roll(g, LANES - 2 * M_HEADS, 1)

        r = lax.broadcasted_iota(jnp.int32, (CHUNK, CHUNK), 0)
        c = lax.broadcasted_iota(jnp.int32, (CHUNK, CHUNK), 1)
        ge = jnp.where(r >= c, 1.0, 0.0).astype(BF16)
        le = jnp.where(r <= c, 1.0, 0.0).astype(BF16)
        fwd_lane = lax.broadcasted_iota(jnp.int32, (CHUNK, LANES), 1) < M_HEADS
        fwd_row = lax.broadcasted_iota(jnp.int32, (2 * SUBLANES, CHUNK), 0) < M_HEADS

        for ci in range(tm // CHUNK):
            sl = slice(ci * CHUNK, (ci + 1) * CHUNK)
            hi, lo = _split_hi_lo(lf_c[sl])
            b_c = jnp.where(fwd_lane, _dot(ge, hi) + _dot(ge, lo), _dot(le, hi) + _dot(le, lo))
            cc_ref[sl, :] = (b_c - gi_c[sl])[:, :2 * M_HEADS]
            g_r = g[sl].T[0:2 * SUBLANES, :]
            hi, lo = _split_hi_lo(jax.nn.log_sigmoid(g_r))
            b_r = jnp.where(fwd_row, _dot(hi, le) + _dot(lo, le), _dot(hi, ge) + _dot(lo, ge))[0:SUBLANES]
            br_ref[:, sl] = b_r
            cr_ref[:, sl] = b_r - g_r[SUBLANES:, :]

    conv = _conv31_tasks(ucat_ref.at[slot_conv], ush_ref, cw_ref, cb_ref, cg_ref, cbeta_ref, uc_ref, tm)
    conv[0]()
    _interleave([glu, k_proj, o_proj, q_proj, v_proj, gates], conv[1:])


def _in_proj(x, seq, w_tok, b_tok, w_fm, b_fm, w_gate, b_gate, conv_w):
    t = x.shape[0]
    tm = IN_TM
    nblk = t // tm
    cur = lambda i: jnp.minimum(i, nblk - 1)
    tok_spec = lambda n: pl.BlockSpec((tm, n), lambda i: (cur(i), 0))
    fm_spec = lambda n: pl.BlockSpec((n, tm), lambda i: (0, cur(i)))
    consts = (w_tok, b_tok, w_fm, b_fm, w_gate, b_gate) + tuple(conv_w)
    ucat_rows = tm + 2 * HALO_ROWS
    return pl.pallas_call(
        functools.partial(_in_proj_kernel, blocks_per_seq=seq // tm),
        grid=(nblk + CONV_DELAY,),
        in_specs=[tok_spec(D_MODEL)] + [_const_spec(a.shape) for a in consts],
        out_specs=[pl.BlockSpec((tm, CONV_W), lambda i: (jnp.maximum(i - CONV_DELAY, 0), 0))] + [tok_spec(CONV_W)] * 2
                  + [fm_spec(M_W)] * 2 + [tok_spec(2 * M_HEADS)] + [fm_spec(SUBLANES)] * 2,
        out_shape=[jax.ShapeDtypeStruct((t, CONV_W), BF16)] * 3 + [jax.ShapeDtypeStruct((M_W, t), BF16)] * 2
                  + [jax.ShapeDtypeStruct((t, 2 * M_HEADS), F32)] + [jax.ShapeDtypeStruct((SUBLANES, t), F32)] * 2,
        scratch_shapes=[pltpu.VMEM((CONV_DELAY + 1, ucat_rows, CONV_W), F32),
                        pltpu.VMEM((SUBLANES - 1, ucat_rows - SUBLANES, CONV_W), F32)],
        compiler_params=_params("arbitrary"),
        name="in_proj",
    )(x, *consts)


def _mlstm_kernel(kf_ref, qtf_ref, vtf_ref, ccf_ref, brf_ref, crf_ref,
                  kb_ref, qtb_ref, vtb_ref, ccb_ref, brb_ref, crb_ref,
                  hf_ref, hb_ref, cn_ref, m_ref):
    @pl.when(pl.program_id(1) == 0)
    def _():
        cn_ref[...] = jnp.zeros_like(cn_ref)
        m_ref[...] = jnp.zeros_like(m_ref)

    r = lax.broadcasted_iota(jnp.int32, (CHUNK, CHUNK), 0)
    c = lax.broadcasted_iota(jnp.int32, (CHUNK, CHUNK), 1)
    masks = (r <= c, r >= c)
    ones_rows = jnp.where(lax.broadcasted_iota(jnp.int32, (M_DH, CHUNK), 0) == 0, 1.0, 0.0).astype(BF16)
    refs = ((kf_ref, qtf_ref, vtf_ref, ccf_ref, brf_ref, crf_ref, hf_ref),
            (kb_ref, qtb_ref, vtb_ref, ccb_ref, brb_ref, crb_ref, hb_ref))
    def load_jobs(step):
        jobs = []
        for dirn in range(2):
            k_ref, qt_ref, vt_ref, cc_ref, br_ref, cr_ref, h_ref = refs[dirn]
            ci = step if dirn == 0 else SCAN_CHUNKS - 1 - step
            sl = slice(ci * CHUNK, (ci + 1) * CHUNK)
            end = CHUNK - 1 if dirn == 0 else 0
            for hd in range(M_HEADS):
                idx = dirn * M_HEADS + hd
                hs = slice(hd * M_DH, (hd + 1) * M_DH)
                b_row = br_ref[idx:idx + 1, sl]
                jobs.append(dict(
                    idx=idx, out=(h_ref, sl, hs), mask=masks[dirn], k=k_ref[sl, hs], qt=qt_ref[hs, sl],
                    vt_aug=jnp.concatenate([vt_ref[hs, sl], ones_rows], axis=0), c_col=cc_ref[sl, idx:idx + 1],
                    b_row=b_row, c_row=cr_ref[idx:idx + 1, sl], b_end=b_row[:, end:end + 1]))
        return jobs

    def carry_free(jobs):
        for j in jobs:
            j["qk"] = _dot(j["k"], j["qt"])
            w_log = j["b_end"] - j["c_row"]
            j["w_max"] = jnp.max(w_log, axis=1, keepdims=True)
            vw = (j["vt_aug"].astype(F32) * jnp.exp(w_log - j["w_max"])).astype(BF16)
            j["kv"] = _dot(vw, j["k"])

    def decay_weights(jobs):
        for j in jobs:
            m_prev = m_ref[j["idx"]][0:1, :]
            d = jnp.where(j["mask"], j["b_row"] - j["c_col"], -jnp.inf)
            a = j["b_row"] + m_prev
            m_t = jnp.maximum(a, jnp.max(d, axis=0, keepdims=True))
            j["m_t"] = m_t
            j["s"] = (j["qk"] * jnp.exp(d - m_t)).astype(BF16)
            j["q_carry"] = (j["qt"].astype(F32) * jnp.exp(a - m_t)).astype(BF16)
            j["a_end"] = j["b_end"] + m_prev

    def outputs_and_state(jobs):
        for j in jobs:
            cn_prev = cn_ref[j["idx"]]
            j["nd"] = _dot(jnp.concatenate([cn_prev.astype(BF16), j["vt_aug"]], axis=1),
                           jnp.concatenate([j["q_carry"], j["s"]], axis=0))
            m_new = jnp.maximum(j["a_end"], j["w_max"])
            cn_ref[j["idx"]] = jnp.exp(j["a_end"] - m_new) * cn_prev + jnp.exp(j["w_max"] - m_new) * j["kv"]
            m_ref[j["idx"]] = jnp.broadcast_to(m_new, m_ref.shape[1:])
        for j in jobs:
            nd = j["nd"]
            den = jnp.maximum(jnp.abs(nd[M_DH:M_DH + 1, :]), jnp.exp(-j["m_t"]))
            h_ref, sl, hs = j["out"]
            h_ref[sl, hs] = (nd[:M_DH, :] / den).T.astype(BF16)

    jobs = load_jobs(0)
    carry_free(jobs)
    for step in range(SCAN_CHUNKS):
        decay_weights(jobs)
        nxt = load_jobs(step + 1) if step + 1 < SCAN_CHUNKS else None
        if nxt is not None:
            carry_free(nxt)
        outputs_and_state(jobs)
        jobs = nxt


def _mlstm_scan(k, qt, vt, c_col, b_row, c_row, nb, seq):
    t = k.shape[0]
    bt = SCAN_CHUNKS * CHUNK
    n = seq // bt
    fwd = lambda b, i: (b * n + i, 0)
    bwd = lambda b, i: (b * n + n - 1 - i, 0)
    fwd_r = lambda b, i: (0, b * n + i)
    bwd_r = lambda b, i: (0, b * n + n - 1 - i)

    def specs(tok_map, fm_map):
        return [pl.BlockSpec((bt, M_W), tok_map)] + [pl.BlockSpec((M_W, bt), fm_map)] * 2 \
               + [pl.BlockSpec((bt, 2 * M_HEADS), tok_map)] + [pl.BlockSpec((SUBLANES, bt), fm_map)] * 2

    args = (k, qt, vt, c_col, b_row, c_row)
    return pl.pallas_call(
        _mlstm_kernel,
        grid=(nb, n),
        in_specs=specs(fwd, fwd_r) + specs(bwd, bwd_r),
        out_specs=[pl.BlockSpec((bt, M_W), fwd), pl.BlockSpec((bt, M_W), bwd)],
        out_shape=[jax.ShapeDtypeStruct((t, M_W), BF16)] * 2,
        scratch_shapes=[pltpu.VMEM((2 * M_HEADS, 2 * M_DH, M_DH), F32),
                        pltpu.VMEM((2 * M_HEADS, SUBLANES, LANES), F32)],
        compiler_params=_params("parallel", "arbitrary"),
        name="mlstm_scan",
    )(*args, *args)


def _mix_attn_kernel(x_ref, uc_ref, hf_ref, hb_ref, o_ref, wo1_ref, bo1_ref, g1_ref, b1_ref,
                     wq_ref, kt_ref, v_ref, wo2_ref, g2_ref, b2_ref, out_ref):
    hm = (hf_ref[...].astype(F32) + hb_ref[...].astype(F32)) * o_ref[...].astype(F32)
    mix = _dot(uc_ref[...], wo1_ref[0:CONV_W, :]) + _dot(hm.astype(BF16), wo1_ref[CONV_W:, :]) + bo1_ref[...]
    x1 = _layer_norm(ALPHA * x_ref[...] + mix, g1_ref[...], b1_ref[...])

    qx = _dot(x1.astype(BF16), wq_ref[...]).astype(BF16)
    heads = []
    for hd in range(X_HEADS):
        hs = slice(hd * X_DH, (hd + 1) * X_DH)
        sc = _dot(qx[:, hs], kt_ref[0, hs, :]) * (X_DH ** -0.5)
        sc = sc - jnp.max(sc, axis=-1, keepdims=True)
        e = jnp.exp(sc)
        pr = e / jnp.sum(e, axis=-1, keepdims=True)
        heads.append(_dot(pr.astype(BF16), v_ref[0, :, hs]).astype(BF16))
    att = _dot(jnp.concatenate(heads, axis=1), wo2_ref[...])
    out_ref[...] = _layer_norm(ALPHA * x1 + att, g2_ref[...], b2_ref[...])


def _mix_attn(x, uc, hf, hb, o, kt, v, weights, nb, seq):
    t = x.shape[0]
    tm = MIX_TM
    n = seq // tm
    tok = lambda b, i: (b * n + i, 0)
    (wo1, bo1, g1, b1, wq, wo2, g2, b2) = weights
    cs = _const_spec
    return pl.pallas_call(
        _mix_attn_kernel,
        grid=(nb, n),
        in_specs=[pl.BlockSpec((tm, D_MODEL), tok)] + [pl.BlockSpec((tm, M_W), tok)] * 4
                 + [cs(wo1.shape), cs(bo1.shape), cs(g1.shape), cs(b1.shape), cs(wq.shape),
                    pl.BlockSpec((1, D_MODEL, N_MEM), lambda b, i: (b, 0, 0)),
                    pl.BlockSpec((1, N_MEM, D_MODEL), lambda b, i: (b, 0, 0)),
                    cs(wo2.shape), cs(g2.shape), cs(b2.shape)],
        out_specs=pl.BlockSpec((tm, D_MODEL), tok),
        out_shape=jax.ShapeDtypeStruct((t, D_MODEL), F32),
        compiler_params=_params("parallel", "parallel"),
        name="mix_attn",
    )(x, uc, hf, hb, o, wo1, bo1, g1, b1, wq, kt, v, wo2, g2, b2)


def _ffn_halo_kernel(x_ref, w_ref, b_ref, out_ref):
    out_ref[...] = _dot(x_ref[...].astype(BF16), w_ref[...]) + b_ref[...]


def _ffn_halo(rows, w_gate, b_gate):
    r = rows.shape[0]
    return pl.pallas_call(
        _ffn_halo_kernel,
        grid=(1,),
        in_specs=[_const_spec(rows.shape), _const_spec(w_gate.shape), _const_spec(b_gate.shape)],
        out_specs=_const_spec((r, D_FF)),
        out_shape=jax.ShapeDtypeStruct((r, D_FF), F32),
        compiler_params=_params("arbitrary"),
        name="ffn_halo",
    )(rows, w_gate, b_gate)


def _conv_ffn_kernel(x_ref, halo_ref, wg_ref, bg_ref, wv_ref, bv_ref, cw_ref, cb_ref, wd_ref, bd_ref,
                     g3_ref, b3_ref, out_ref):
    i = pl.program_id(1)
    n = pl.num_programs(1)
    tm = FFN_TM
    x = x_ref[...]
    xb = x.astype(BF16)
    row = lax.broadcasted_iota(jnp.int32, (tm, 1), 0)
    halo = halo_ref[0]
    w = D_FF // FFN_SPLITS
    cols = [slice(s * w, (s + 1) * w) for s in range(FFN_SPLITS)]

    def up(sl):
        return _dot(xb, wg_ref[:, sl]) + bg_ref[:, sl], _dot(xb, wv_ref[:, sl]) + bv_ref[:, sl]

    def gated(sl, gp, val):
        before = jnp.where(i > 0, halo[0:1, sl], 0.0)
        after = jnp.where(i < n - 1, halo[1:2, sl], 0.0)
        g_prev = jnp.where(row == 0, before, pltpu.roll(gp, 1, 0))
        g_next = jnp.where(row == tm - 1, after, pltpu.roll(gp, tm - 1, 0))
        gate = g_prev * cw_ref[0:1, sl] + gp * cw_ref[1:2, sl] + g_next * cw_ref[2:3, sl] + cb_ref[:, sl]
        return (jax.nn.gelu(gate) * val).astype(BF16)

    acc = jnp.zeros((tm, D_MODEL), F32) + bd_ref[...]
    cur = up(cols[0])
    for s, sl in enumerate(cols):
        nxt = up(cols[s + 1]) if s + 1 < FFN_SPLITS else None
        acc = acc + _dot(gated(sl, *cur), wd_ref[sl, :])
        cur = nxt
    out_ref[...] = _layer_norm(ALPHA * x + acc, g3_ref[...], b3_ref[...])


def _conv_ffn(x2, weights, nb, seq):
    t = x2.shape[0]
    tm = FFN_TM
    n = seq // tm
    (wg, bg, wv, bv, cw, cb, wd, bd, g3, b3) = weights
    xr = x2.reshape(nb, n, tm, D_MODEL)
    last = xr[:, :, tm - 1]
    first = xr[:, :, 0]
    before = jnp.concatenate([last[:, :1], last[:, :-1]], axis=1)
    after = jnp.concatenate([first[:, 1:], first[:, -1:]], axis=1)
    rows = jnp.stack([before, after], axis=2).reshape(nb * n * 2, D_MODEL)
    halo = _ffn_halo(rows, wg, bg).reshape(nb * n, 2, D_FF)

    tok = lambda b, i: (b * n + i, 0)
    cs = _const_spec
    return pl.pallas_call(
        _conv_ffn_kernel,
        grid=(nb, n),
        in_specs=[pl.BlockSpec((tm, D_MODEL), tok),
                  pl.BlockSpec((1, 2, D_FF), lambda b, i: (b * n + i, 0, 0)),
                  cs(wg.shape), cs(bg.shape), cs(wv.shape), cs(bv.shape), cs(cw.shape), cs(cb.shape),
                  cs(wd.shape), cs(bd.shape), cs(g3.shape), cs(b3.shape)],
        out_specs=pl.BlockSpec((tm, D_MODEL), tok),
        out_shape=jax.ShapeDtypeStruct((t, D_MODEL), F32),
        compiler_params=_params("parallel", "parallel"),
        name="conv_ffn",
    )(x2, halo, wg, bg, wv, bv, cw, cb, wd, bd, g3, b3)


def _pad_cols(w, n):
    return jnp.pad(w, ((0, 0), (0, n - w.shape[1])))


def _prep_layer(w_in, b_in, conv_w, conv_b, cln_g, cln_b, w_out, b_out, ln1_g, ln1_b, xq_w, xkv_w, xo_w,
                ln2_g, ln2_b, up_w, up_b, fconv_w, fconv_b, down_w, down_b, ln3_g, ln3_b):
    row = lambda a: a.reshape(1, -1)
    wg = w_in[:, N_MAIN:].reshape(D_MODEL, 4, M_HEADS)
    bg = b_in[N_MAIN:].reshape(4, M_HEADS)
    gate_order = (1, 3, 0, 2)
    w_gate = _pad_cols(jnp.concatenate([wg[:, j] for j in gate_order], axis=1), LANES).astype(BF16)
    b_gate = _pad_cols(row(jnp.concatenate([bg[j] for j in gate_order])), LANES)
    conv_in = (jnp.broadcast_to(conv_w[:, None, :], (CONV_K, SUBLANES, CONV_W)), row(conv_b), row(cln_g), row(cln_b))
    seg = lambda a, j: a[..., j * CONV_W:(j + 1) * CONV_W]
    tok_segs, fm_segs = (0, 1, 3, 5), (2, 4)
    in_w = (jnp.concatenate([seg(w_in, j) for j in tok_segs], axis=1).astype(BF16),
            row(jnp.concatenate([seg(b_in, j) for j in tok_segs])),
            jnp.concatenate([seg(w_in, j).T for j in fm_segs], axis=0).astype(BF16),
            jnp.concatenate([seg(b_in, j) for j in fm_segs]).reshape(-1, 1),
            w_gate, b_gate, conv_in)
    kv_w = (xkv_w[:, :D_MODEL].T.astype(BF16), xkv_w[:, D_MODEL:].astype(BF16))
    mix_w = (w_out.astype(BF16), row(b_out), row(ln1_g), row(ln1_b),
             xq_w.astype(BF16), xo_w.astype(BF16), row(ln2_g), row(ln2_b))
    ffn_w = (up_w[:, :D_FF].astype(BF16), row(up_b[:D_FF]), up_w[:, D_FF:].astype(BF16), row(up_b[D_FF:]),
             fconv_w, row(fconv_b), down_w.astype(BF16), row(down_b), row(ln3_g), row(ln3_b))
    return in_w, kv_w, mix_w, ffn_w


def _encoder_layer(x, mem, prepped):
    in_w, kv_w, mix_w, ffn_w = prepped
    nb, seq, _ = x.shape
    xf = x.reshape(nb * seq, D_MODEL)
    kt, v_mem = _kv_proj(mem, *kv_w)
    uc, k, o, qt, vt, c_col, b_row, c_row = _in_proj(xf, seq, *in_w)
    hf, hb = _mlstm_scan(k, qt, vt, c_col, b_row, c_row, nb, seq)
    x2 = _mix_attn(xf, uc, hf, hb, o, kt, v_mem, mix_w, nb, seq)
    x3 = _conv_ffn(x2, ffn_w, nb, seq)
    return x3.reshape(nb, seq, D_MODEL)


def kernel(x_prompt, x_sample, mem_prompt, mem_sample, w_in, b_in, conv_w, conv_b, cln_g, cln_b, w_out, b_out,
           ln1_g, ln1_b, xq_w, xkv_w, xo_w, ln2_g, ln2_b, up_w, up_b, fconv_w, fconv_b, down_w, down_b,
           ln3_g, ln3_b):
    layer_w = (w_in, b_in, conv_w, conv_b, cln_g, cln_b, w_out, b_out, ln1_g, ln1_b, xq_w, xkv_w, xo_w,
               ln2_g, ln2_b, up_w, up_b, fconv_w, fconv_b, down_w, down_b, ln3_g, ln3_b)
    y_prompt, y_sample = x_prompt, x_sample
    for l in range(DEPTH):
        prepped = _prep_layer(*[w[l] for w in layer_w])
        y_prompt = _encoder_layer(y_prompt, mem_prompt, prepped)
        y_sample = _encoder_layer(y_sample, mem_sample, prepped)
    return (y_prompt, y_sample)
```

```python
import functools

import numpy as np
import jax
import jax.numpy as jnp
from jax import lax
from jax.experimental import pallas as pl
from jax.experimental.pallas import tpu as pltpu

D_MODEL = 1024
DEPTH = 1
CONV_W = D_MODEL // 2
CONV_K = 31
CONV_PAD = CONV_K // 2
M_HEADS = 4
M_W = D_MODEL // 2
M_DH = M_W // M_HEADS
CHUNK = 128
N_MAIN = 2 * CONV_W + 4 * M_W
N_MEM = 256
X_HEADS = 4
X_DH = D_MODEL // X_HEADS
D_FF = 2816
FFN_K = 3
ALPHA = (2.0 * DEPTH) ** 0.25
LN_EPS = 1e-5

F32 = jnp.float32
BF16 = jnp.bfloat16

VMEM_LIMIT_BYTES = 56 * 1024 * 1024
LANES = 128
SUBLANES = 8
HALO_ROWS = 16
CONV_ROWS = 32
CONV_DELAY = 2

IN_TM = 512
SCAN_CHUNKS = 4
MIX_TM = 512
MIX_ROWS = 128
FFN_TM = 512
FFN_ROWS = 256
FFN_SPLITS = 2


def _params(*sem):
    return pltpu.CompilerParams(dimension_semantics=sem, vmem_limit_bytes=VMEM_LIMIT_BYTES)


def _dot(a, b):
    return jnp.dot(a, b, preferred_element_type=F32)


def _dot_nt(a, b):
    return lax.dot_general(a, b, (((1,), (1,)), ((), ())), preferred_element_type=F32)


def _layer_norm(x, g, b):
    mu = jnp.mean(x, axis=-1, keepdims=True)
    xc = x - mu
    var = jnp.mean(xc * xc, axis=-1, keepdims=True)
    return xc * lax.rsqrt(var + LN_EPS) * g + b


def _const_spec(shape):
    return pl.BlockSpec(shape, lambda *_: (0,) * len(shape))


def _kv_kernel(mem_ref, wkt_ref, wv_ref, kt_ref, v_ref):
    mem = mem_ref[0].astype(BF16)
    kt_ref[0] = _dot_nt(wkt_ref[...], mem).astype(BF16)
    v_ref[0] = _dot(mem, wv_ref[...]).astype(BF16)


def _kv_proj(mem, wkt, wv):
    nb = mem.shape[0]
    return pl.pallas_call(
        _kv_kernel,
        grid=(nb,),
        in_specs=[pl.BlockSpec((1, N_MEM, D_MODEL), lambda b: (b, 0, 0)),
                  _const_spec((D_MODEL, D_MODEL)), _const_spec((D_MODEL, D_MODEL))],
        out_specs=[pl.BlockSpec((1, D_MODEL, N_MEM), lambda b: (b, 0, 0)),
                   pl.BlockSpec((1, N_MEM, D_MODEL), lambda b: (b, 0, 0))],
        out_shape=[jax.ShapeDtypeStruct((nb, D_MODEL, N_MEM), BF16),
                   jax.ShapeDtypeStruct((nb, N_MEM, D_MODEL), BF16)],
        compiler_params=_params("parallel"),
        name="kv_proj",
    )(mem, wkt, wv)


def _split_hi_lo(x):
    hi = x.astype(BF16)
    lo = (x - hi.astype(F32)).astype(BF16)
    return hi, lo


def _conv31_tasks(ucat_ref, ush_ref, cw_ref, cb_ref, cg_ref, cbeta_ref, out_ref, tm):
    n_sh = ush_ref.shape[1]

    def shifted_copies():
        x = ucat_ref[...]
        x = x.reshape(x.shape[0] // SUBLANES, SUBLANES, CONV_W)
        sub = lax.broadcasted_iota(jnp.int32, (1, SUBLANES, CONV_W), 1)
        for r in range(1, SUBLANES):
            rot = pltpu.roll(x, SUBLANES - r, 1)
            ush_ref[r - 1] = jnp.where(sub < SUBLANES - r, rot[:-1], rot[1:]).reshape(n_sh, CONV_W)

    groups = CONV_ROWS // SUBLANES

    def rows(r0):
        acc = jnp.zeros((groups, SUBLANES, CONV_W), F32) + cb_ref[...]
        for j in range(CONV_K):
            a, r = divmod(HALO_ROWS - CONV_PAD + j, SUBLANES)
            src = ucat_ref if r == 0 else ush_ref.at[r - 1]
            lo = r0 + a * SUBLANES
            acc = acc + src[lo:lo + CONV_ROWS, :].reshape(groups, SUBLANES, CONV_W) * cw_ref[j]
        uc = jax.nn.silu(_layer_norm(acc.reshape(CONV_ROWS, CONV_W), cg_ref[...], cbeta_ref[...]))
        out_ref[r0:r0 + CONV_ROWS, :] = uc.astype(BF16)

    return [shifted_copies] + [functools.partial(rows, r0) for r0 in range(0, tm, CONV_ROWS)]


def _interleave(main, filler):
    done = 0
    for n, task in enumerate(main):
        task()
        upto = (n + 1) * len(filler) // len(main)
        for f in filler[done:upto]:
            f()
        done = upto


def _in_proj_kernel(x_ref, w_ref, b_ref, wt_ref, bt_ref, wg_ref, bg_ref, cw_ref, cb_ref, cg_ref, cbeta_ref,
                    uc_ref, k_ref, o_ref, qt_ref, vt_ref, cc_ref, br_ref, cr_ref, ucat_ref, ush_ref,
                    *, blocks_per_seq):
    i = pl.program_id(0)
    n_blocks = pl.num_programs(0) - CONV_DELAY
    tm = IN_TM
    slot_new = lax.rem(i, CONV_DELAY + 1)
    slot_prev = lax.rem(i + CONV_DELAY, CONV_DELAY + 1)
    slot_conv = lax.rem(i + 1, CONV_DELAY + 1)

    @pl.when(i == 0)
    def _():
        ucat_ref[...] = jnp.zeros_like(ucat_ref)

    xb = x_ref[...].astype(BF16)
    starts_seq = lax.rem(i, blocks_per_seq) == 0

    def seg(j):
        sl = slice(j * CONV_W, (j + 1) * CONV_W)
        return _dot(xb, w_ref[:, sl]) + b_ref[:, sl]

    def glu():
        u = seg(0) * jax.nn.sigmoid(seg(1))
        ucat_ref[slot_new, HALO_ROWS:HALO_ROWS + tm, :] = u
        ucat_ref[slot_new, 0:HALO_ROWS, :] = jnp.where(starts_seq, 0.0, ucat_ref[slot_prev, tm:tm + HALO_ROWS, :])
        ucat_ref[slot_prev, HALO_ROWS + tm:, :] = jnp.where(jnp.logical_or(starts_seq, i >= n_blocks), 0.0,
                                                            u[0:HALO_ROWS, :])

    def k_proj():
        k_ref[...] = (seg(2) * (M_DH ** -0.5)).astype(BF16)

    def o_proj():
        o_ref[...] = jax.nn.sigmoid(seg(3)).astype(BF16)

    def q_proj():
        qt_ref[...] = (_dot_nt(wt_ref[0:M_W, :], xb) + bt_ref[0:M_W, :]).astype(BF16)

    def v_proj():
        vt_ref[...] = (_dot_nt(wt_ref[M_W:, :], xb) + bt_ref[M_W:, :]).astype(BF16)

    def gates():
        g = _dot(xb, wg_ref[...]) + bg_ref[...]
        lf_c = jax.nn.log_sigmoid(g)
        gi_c = pltpu.roll(g, LANES - 2 * M_HEADS, 1)

        r = lax.broadcasted_iota(jnp.int32, (CHUNK, CHUNK), 0)
        c = lax.broadcasted_iota(jnp.int32, (CHUNK, CHUNK), 1)
        ge = jnp.where(r >= c, 1.0, 0.0).astype(BF16)
        le = jnp.where(r <= c, 1.0, 0.0).astype(BF16)
        fwd_lane = lax.broadcasted_iota(jnp.int32, (CHUNK, LANES), 1) < M_HEADS
        fwd_row = lax.broadcasted_iota(jnp.int32, (2 * SUBLANES, CHUNK), 0) < M_HEADS

        for ci in range(tm // CHUNK):
            sl = slice(ci * CHUNK, (ci + 1) * CHUNK)
            hi, lo = _split_hi_lo(lf_c[sl])
            b_c = jnp.where(fwd_lane, _dot(ge, hi) + _dot(ge, lo), _dot(le, hi) + _dot(le, lo))
            cc_ref[sl, :] = (b_c - gi_c[sl])[:, :2 * M_HEADS]
            g_r = g[sl].T[0:2 * SUBLANES, :]
            hi, lo = _split_hi_lo(jax.nn.log_sigmoid(g_r))
            b_r = jnp.where(fwd_row, _dot(hi, le) + _dot(lo, le), _dot(hi, ge) + _dot(lo, ge))[0:SUBLANES]
            br_ref[:, sl] = b_r
            cr_ref[:, sl] = b_r - g_r[SUBLANES:, :]

    conv = _conv31_tasks(ucat_ref.at[slot_conv], ush_ref, cw_ref, cb_ref, cg_ref, cbeta_ref, uc_ref, tm)
    conv[0]()
    _interleave([glu, k_proj, o_proj, q_proj, v_proj, gates], conv[1:])


def _in_proj(x, seq, w_tok, b_tok, w_fm, b_fm, w_gate, b_gate, conv_w):
    t = x.shape[0]
    tm = IN_TM
    nblk = t // tm
    cur = lambda i: jnp.minimum(i, nblk - 1)
    tok_spec = lambda n: pl.BlockSpec((tm, n), lambda i: (cur(i), 0))
    fm_spec = lambda n: pl.BlockSpec((n, tm), lambda i: (0, cur(i)))
    consts = (w_tok, b_tok, w_fm, b_fm, w_gate, b_gate) + tuple(conv_w)
    ucat_rows = tm + 2 * HALO_ROWS
    return pl.pallas_call(
        functools.partial(_in_proj_kernel, blocks_per_seq=seq // tm),
        grid=(nblk + CONV_DELAY,),
        in_specs=[tok_spec(D_MODEL)] + [_const_spec(a.shape) for a in consts],
        out_specs=[pl.BlockSpec((tm, CONV_W), lambda i: (jnp.maximum(i - CONV_DELAY, 0), 0))] + [tok_spec(CONV_W)] * 2
                  + [fm_spec(M_W)] * 2 + [tok_spec(2 * M_HEADS)] + [fm_spec(SUBLANES)] * 2,
        out_shape=[jax.ShapeDtypeStruct((t, CONV_W), BF16)] * 3 + [jax.ShapeDtypeStruct((M_W, t), BF16)] * 2
                  + [jax.ShapeDtypeStruct((t, 2 * M_HEADS), F32)] + [jax.ShapeDtypeStruct((SUBLANES, t), F32)] * 2,
        scratch_shapes=[pltpu.VMEM((CONV_DELAY + 1, ucat_rows, CONV_W), F32),
                        pltpu.VMEM((SUBLANES - 1, ucat_rows - SUBLANES, CONV_W), F32)],
        compiler_params=_params("arbitrary"),
        name="in_proj",
    )(x, *consts)


def _mlstm_kernel(kf_ref, qtf_ref, vtf_ref, ccf_ref, brf_ref, crf_ref,
                  kb_ref, qtb_ref, vtb_ref, ccb_ref, brb_ref, crb_ref,
                  hf_ref, hb_ref, cn_ref, m_ref):
    @pl.when(pl.program_id(1) == 0)
    def _():
        cn_ref[...] = jnp.zeros_like(cn_ref)
        m_ref[...] = jnp.zeros_like(m_ref)

    r = lax.broadcasted_iota(jnp.int32, (CHUNK, CHUNK), 0)
    c = lax.broadcasted_iota(jnp.int32, (CHUNK, CHUNK), 1)
    masks = (r <= c, r >= c)
    ones_rows = jnp.where(lax.broadcasted_iota(jnp.int32, (M_DH, CHUNK), 0) == 0, 1.0, 0.0).astype(BF16)
    refs = ((kf_ref, qtf_ref, vtf_ref, ccf_ref, brf_ref, crf_ref, hf_ref),
            (kb_ref, qtb_ref, vtb_ref, ccb_ref, brb_ref, crb_ref, hb_ref))
    def load_jobs(step):
        jobs = []
        for dirn in range(2):
            k_ref, qt_ref, vt_ref, cc_ref, br_ref, cr_ref, h_ref = refs[dirn]
            ci = step if dirn == 0 else SCAN_CHUNKS - 1 - step
            sl = slice(ci * CHUNK, (ci + 1) * CHUNK)
            end = CHUNK - 1 if dirn == 0 else 0
            for hd in range(M_HEADS):
                idx = dirn * M_HEADS + hd
                hs = slice(hd * M_DH, (hd + 1) * M_DH)
                b_row = br_ref[idx:idx + 1, sl]
                jobs.append(dict(
                    idx=idx, out=(h_ref, sl, hs), mask=masks[dirn], k=k_ref[sl, hs], qt=qt_ref[hs, sl],
                    vt_aug=jnp.concatenate([vt_ref[hs, sl], ones_rows], axis=0), c_col=cc_ref[sl, idx:idx + 1],
                    b_row=b_row, c_row=cr_ref[idx:idx + 1, sl], b_end=b_row[:, end:end + 1]))
        return jobs

    def carry_free(jobs):
        for j in jobs:
            j["qk"] = _dot(j["k"], j["qt"])
            w_log = j["b_end"] - j["c_row"]
            j["w_max"] = jnp.max(w_log, axis=1, keepdims=True)
            vw = (j["vt_aug"].astype(F32) * jnp.exp(w_log - j["w_max"])).astype(BF16)
            j["kv"] = _dot(vw, j["k"])

    def decay_weights(jobs):
        for j in jobs:
            m_prev = m_ref[j["idx"]][0:1, :]
            d = jnp.where(j["mask"], j["b_row"] - j["c_col"], -jnp.inf)
            a = j["b_row"] + m_prev
            m_t = jnp.maximum(a, jnp.max(d, axis=0, keepdims=True))
            j["m_t"] = m_t
            j["s"] = (j["qk"] * jnp.exp(d - m_t)).astype(BF16)
            j["q_carry"] = (j["qt"].astype(F32) * jnp.exp(a - m_t)).astype(BF16)
            j["a_end"] = j["b_end"] + m_prev

    def outputs_and_state(jobs):
        for j in jobs:
            cn_prev = cn_ref[j["idx"]]
            j["nd"] = _dot(jnp.concatenate([cn_prev.astype(BF16), j["vt_aug"]], axis=1),
                           jnp.concatenate([j["q_carry"], j["s"]], axis=0))
            m_new = jnp.maximum(j["a_end"], j["w_max"])
            cn_ref[j["idx"]] = jnp.exp(j["a_end"] - m_new) * cn_prev + jnp.exp(j["w_max"] - m_new) * j["kv"]
            m_ref[j["idx"]] = jnp.broadcast_to(m_new, m_ref.shape[1:])
        for j in jobs:
            nd = j["nd"]
            den = jnp.maximum(jnp.abs(nd[M_DH:M_DH + 1, :]), jnp.exp(-j["m_t"]))
            h_ref, sl, hs = j["out"]
            h_ref[sl, hs] = (nd[:M_DH, :] / den).T.astype(BF16)

    jobs = load_jobs(0)
    carry_free(jobs)
    for step in range(SCAN_CHUNKS):
        decay_weights(jobs)
        nxt = load_jobs(step + 1) if step + 1 < SCAN_CHUNKS else None
        if nxt is not None:
            carry_free(nxt)
        outputs_and_state(jobs)
        jobs = nxt


def _mlstm_scan(k, qt, vt, c_col, b_row, c_row, nb, seq):
    t = k.shape[0]
    bt = SCAN_CHUNKS * CHUNK
    n = seq // bt
    fwd = lambda b, i: (b * n + i, 0)
    bwd = lambda b, i: (b * n + n - 1 - i, 0)
    fwd_r = lambda b, i: (0, b * n + i)
    bwd_r = lambda b, i: (0, b * n + n - 1 - i)

    def specs(tok_map, fm_map):
        return [pl.BlockSpec((bt, M_W), tok_map)] + [pl.BlockSpec((M_W, bt), fm_map)] * 2 \
               + [pl.BlockSpec((bt, 2 * M_HEADS), tok_map)] + [pl.BlockSpec((SUBLANES, bt), fm_map)] * 2

    args = (k, qt, vt, c_col, b_row, c_row)
    return pl.pallas_call(
        _mlstm_kernel,
        grid=(nb, n),
        in_specs=specs(fwd, fwd_r) + specs(bwd, bwd_r),
        out_specs=[pl.BlockSpec((bt, M_W), fwd), pl.BlockSpec((bt, M_W), bwd)],
        out_shape=[jax.ShapeDtypeStruct((t, M_W), BF16)] * 2,
        scratch_shapes=[pltpu.VMEM((2 * M_HEADS, 2 * M_DH, M_DH), F32),
                        pltpu.VMEM((2 * M_HEADS, SUBLANES, LANES), F32)],
        compiler_params=_params("parallel", "arbitrary"),
        name="mlstm_scan",
    )(*args, *args)


def _mix_attn_kernel(x_ref, uc_ref, hf_ref, hb_ref, o_ref, wo1_ref, bo1_ref, g1_ref, b1_ref,
                     wq_ref, kt_ref, v_ref, wo2_ref, g2_ref, b2_ref, out_ref):
    head_cols = [slice(hd * X_DH, (hd + 1) * X_DH) for hd in range(X_HEADS)]

    def out_proj(j):
        sl = j["rows"]
        hm = (hf_ref[sl, :].astype(F32) + hb_ref[sl, :].astype(F32)) * o_ref[sl, :].astype(F32)
        j["mix"] = _dot(jnp.concatenate([uc_ref[sl, :], hm.astype(BF16)], axis=1), wo1_ref[...]) + bo1_ref[...]

    def norm1(j):
        j["x1"] = _layer_norm(ALPHA * x_ref[j["rows"], :] + j["mix"], g1_ref[...], b1_ref[...])

    def q_proj(j):
        j["qx"] = _dot(j["x1"].astype(BF16), wq_ref[...]).astype(BF16)

    def scores(j):
        j["sc"] = [_dot(j["qx"][:, hs], kt_ref[0, hs, :]) * (X_DH ** -0.5) for hs in head_cols]

    def softmax(j):
        probs = []
        for sc in j["sc"]:
            e = jnp.exp(sc - jnp.max(sc, axis=-1, keepdims=True))
            probs.append((e / jnp.sum(e, axis=-1, keepdims=True)).astype(BF16))
        j["pr"] = probs

    def context(j):
        j["ctx"] = jnp.concatenate([_dot(pr, v_ref[0, :, hs]).astype(BF16) for pr, hs in zip(j["pr"], head_cols)],
                                   axis=1)

    def attn_out(j):
        j["att"] = _dot(j["ctx"], wo2_ref[...])

    def norm2(j):
        out_ref[j["rows"], :] = _layer_norm(ALPHA * j["x1"] + j["att"], g2_ref[...], b2_ref[...])

    groups = [dict(rows=slice(r, r + MIX_ROWS)) for r in range(0, MIX_TM, MIX_ROWS)]
    for stage in (out_proj, norm1, q_proj, scores, softmax, context, attn_out, norm2):
        for j in groups:
            stage(j)


def _mix_attn(x, uc, hf, hb, o, kt, v, weights, nb, seq):
    t = x.shape[0]
    tm = MIX_TM
    n = seq // tm
    tok = lambda b, i: (b * n + i, 0)
    (wo1, bo1, g1, b1, wq, wo2, g2, b2) = weights
    cs = _const_spec
    return pl.pallas_call(
        _mix_attn_kernel,
        grid=(nb, n),
        in_specs=[pl.BlockSpec((tm, D_MODEL), tok)] + [pl.BlockSpec((tm, M_W), tok)] * 4
                 + [cs(wo1.shape), cs(bo1.shape), cs(g1.shape), cs(b1.shape), cs(wq.shape),
                    pl.BlockSpec((1, D_MODEL, N_MEM), lambda b, i: (b, 0, 0)),
                    pl.BlockSpec((1, N_MEM, D_MODEL), lambda b, i: (b, 0, 0)),
                    cs(wo2.shape), cs(g2.shape), cs(b2.shape)],
        out_specs=pl.BlockSpec((tm, D_MODEL), tok),
        out_shape=jax.ShapeDtypeStruct((t, D_MODEL), F32),
        compiler_params=_params("parallel", "parallel"),
        name="mix_attn",
    )(x, uc, hf, hb, o, wo1, bo1, g1, b1, wq, kt, v, wo2, g2, b2)


def _ffn_halo_kernel(x_ref, w_ref, b_ref, out_ref):
    out_ref[...] = _dot(x_ref[...].astype(BF16), w_ref[...]) + b_ref[...]


def _ffn_halo(rows, w_gate, b_gate):
    r = rows.shape[0]
    return pl.pallas_call(
        _ffn_halo_kernel,
        grid=(1,),
        in_specs=[_const_spec(rows.shape), _const_spec(w_gate.shape), _const_spec(b_gate.shape)],
        out_specs=_const_spec((r, D_FF)),
        out_shape=jax.ShapeDtypeStruct((r, D_FF), F32),
        compiler_params=_params("arbitrary"),
        name="ffn_halo",
    )(rows, w_gate, b_gate)


def _gelu_tanh_x2(x):
    c = float(np.float32(np.sqrt(2.0 / np.pi)))
    return x * (1.0 + jnp.tanh(x * (c + (c * 0.044715) * (x * x))))


def _conv_ffn_kernel(x_ref, halo_ref, wg_ref, bg_ref, wv_ref, bv_ref, cw_ref, cb_ref, wd_ref, bd_ref,
                     g3_ref, b3_ref, out_ref):
    i = pl.program_id(1)
    n = pl.num_programs(1)
    tm = FFN_TM
    halo = halo_ref[0]
    w = D_FF // FFN_SPLITS
    nrow = tm // FFN_ROWS
    row = lax.broadcasted_iota(jnp.int32, (SUBLANES, 1), 0)
    xs = [x_ref[g * FFN_ROWS:(g + 1) * FFN_ROWS, :] for g in range(nrow)]
    xbs = [x.astype(BF16) for x in xs]
    accs = [jnp.zeros((FFN_ROWS, D_MODEL), F32) + bd_ref[...] for _ in range(nrow)]
    for s in range(FFN_SPLITS):
        sl = slice(s * w, (s + 1) * w)
        gps = [_dot(xb, wg_ref[:, sl]) + bg_ref[:, sl] for xb in xbs]
        vals = [_dot(xb, wv_ref[:, sl]) + bv_ref[:, sl] for xb in xbs]
        acts = []
        for g in range(nrow):
            before = gps[g - 1][FFN_ROWS - 1:, :] if g > 0 else jnp.where(i > 0, halo[0:1, sl], 0.0)
            after = gps[g + 1][0:1, :] if g + 1 < nrow else jnp.where(i < n - 1, halo[1:2, sl], 0.0)
            g_prev = pltpu.roll(gps[g], 1, 0)
            g_prev = jnp.concatenate([jnp.where(row == 0, before, g_prev[0:SUBLANES]), g_prev[SUBLANES:]], axis=0)
            g_next = pltpu.roll(gps[g], FFN_ROWS - 1, 0)
            g_next = jnp.concatenate([g_next[:FFN_ROWS - SUBLANES],
                                      jnp.where(row == SUBLANES - 1, after, g_next[FFN_ROWS - SUBLANES:])], axis=0)
            gate = g_prev * cw_ref[0:1, sl] + gps[g] * cw_ref[1:2, sl] + g_next * cw_ref[2:3, sl] + cb_ref[:, sl]
            acts.append((_gelu_tanh_x2(gate) * vals[g]).astype(BF16))
        accs = [acc + _dot(act, wd_ref[sl, :]) for acc, act in zip(accs, acts)]
    for g in range(nrow):
        out_ref[g * FFN_ROWS:(g + 1) * FFN_ROWS, :] = _layer_norm(ALPHA * xs[g] + accs[g], g3_ref[...], b3_ref[...])


def _conv_ffn(x2, weights, nb, seq):
    t = x2.shape[0]
    tm = FFN_TM
    n = seq // tm
    (wg, bg, wv, bv, cw, cb, wd, bd, g3, b3) = weights
    xr = x2.reshape(nb, n, tm, D_MODEL)
    last = xr[:, :, tm - 1]
    first = xr[:, :, 0]
    before = jnp.concatenate([last[:, :1], last[:, :-1]], axis=1)
    after = jnp.concatenate([first[:, 1:], first[:, -1:]], axis=1)
    rows = jnp.stack([before, after], axis=2).reshape(nb * n * 2, D_MODEL)
    halo = _ffn_halo(rows, wg, bg).reshape(nb * n, 2, D_FF)

    tok = lambda b, i: (b * n + i, 0)
    cs = _const_spec
    return pl.pallas_call(
        _conv_ffn_kernel,
        grid=(nb, n),
        in_specs=[pl.BlockSpec((tm, D_MODEL), tok),
                  pl.BlockSpec((1, 2, D_FF), lambda b, i: (b * n + i, 0, 0)),
                  cs(wg.shape), cs(bg.shape), cs(wv.shape), cs(bv.shape), cs(cw.shape), cs(cb.shape),
                  cs(wd.shape), cs(bd.shape), cs(g3.shape), cs(b3.shape)],
        out_specs=pl.BlockSpec((tm, D_MODEL), tok),
        out_shape=jax.ShapeDtypeStruct((t, D_MODEL), F32),
        compiler_params=_params("parallel", "parallel"),
        name="conv_ffn",
    )(x2, halo, wg, bg, wv, bv, cw, cb, wd, bd, g3, b3)


def _pad_cols(w, n):
    return jnp.pad(w, ((0, 0), (0, n - w.shape[1])))


def _prep_layer(w_in, b_in, conv_w, conv_b, cln_g, cln_b, w_out, b_out, ln1_g, ln1_b, xq_w, xkv_w, xo_w,
                ln2_g, ln2_b, up_w, up_b, fconv_w, fconv_b, down_w, down_b, ln3_g, ln3_b):
    row = lambda a: a.reshape(1, -1)
    wg = w_in[:, N_MAIN:].reshape(D_MODEL, 4, M_HEADS)
    bg = b_in[N_MAIN:].reshape(4, M_HEADS)
    gate_order = (1, 3, 0, 2)
    w_gate = _pad_cols(jnp.concatenate([wg[:, j] for j in gate_order], axis=1), LANES).astype(BF16)
    b_gate = _pad_cols(row(jnp.concatenate([bg[j] for j in gate_order])), LANES)
    conv_in = (jnp.broadcast_to(conv_w[:, None, :], (CONV_K, SUBLANES, CONV_W)), row(conv_b), row(cln_g), row(cln_b))
    seg = lambda a, j: a[..., j * CONV_W:(j + 1) * CONV_W]
    tok_segs, fm_segs = (0, 1, 3, 5), (2, 4)
    in_w = (jnp.concatenate([seg(w_in, j) for j in tok_segs], axis=1).astype(BF16),
            row(jnp.concatenate([seg(b_in, j) for j in tok_segs])),
            jnp.concatenate([seg(w_in, j).T for j in fm_segs], axis=0).astype(BF16),
            jnp.concatenate([seg(b_in, j) for j in fm_segs]).reshape(-1, 1),
            w_gate, b_gate, conv_in)
    kv_w = (xkv_w[:, :D_MODEL].T.astype(BF16), xkv_w[:, D_MODEL:].astype(BF16))
    mix_w = (w_out.astype(BF16), row(b_out), row(ln1_g), row(ln1_b),
             xq_w.astype(BF16), xo_w.astype(BF16), row(ln2_g), row(ln2_b))
    ffn_w = (up_w[:, :D_FF].astype(BF16), row(up_b[:D_FF]), up_w[:, D_FF:].astype(BF16), row(up_b[D_FF:]),
             fconv_w, row(fconv_b), (0.5 * down_w).astype(BF16), row(down_b), row(ln3_g), row(ln3_b))
    return in_w, kv_w, mix_w, ffn_w


def _encoder_layer(x, mem, prepped):
    in_w, kv_w, mix_w, ffn_w = prepped
    nb, seq, _ = x.shape
    xf = x.reshape(nb * seq, D_MODEL)
    kt, v_mem = _kv_proj(mem, *kv_w)
    uc, k, o, qt, vt, c_col, b_row, c_row = _in_proj(xf, seq, *in_w)
    hf, hb = _mlstm_scan(k, qt, vt, c_col, b_row, c_row, nb, seq)
    x2 = _mix_attn(xf, uc, hf, hb, o, kt, v_mem, mix_w, nb, seq)
    x3 = _conv_ffn(x2, ffn_w, nb, seq)
    return x3.reshape(nb, seq, D_MODEL)


def kernel(x_prompt, x_sample, mem_prompt, mem_sample, w_in, b_in, conv_w, conv_b, cln_g, cln_b, w_out, b_out,
           ln1_g, ln1_b, xq_w, xkv_w, xo_w, ln2_g, ln2_b, up_w, up_b, fconv_w, fconv_b, down_w, down_b,
           ln3_g, ln3_b):
    layer_w = (w_in, b_in, conv_w, conv_b, cln_g, cln_b, w_out, b_out, ln1_g, ln1_b, xq_w, xkv_w, xo_w,
               ln2_g, ln2_b, up_w, up_b, fconv_w, fconv_b, down_w, down_b, ln3_g, ln3_b)
    y_prompt, y_sample = x_prompt, x_sample
    for l in range(DEPTH):
        prepped = _prep_layer(*[w[l] for w in layer_w])
        y_prompt = _encoder_layer(y_prompt, mem_prompt, prepped)
        y_sample = _encoder_layer(y_sample, mem_sample, prepped)
    return (y_prompt, y_sample)
```

```python
import functools

import numpy as np
import jax
import jax.numpy as jnp
from jax import lax
from jax.experimental import pallas as pl
from jax.experimental.pallas import tpu as pltpu

D_MODEL = 1024
DEPTH = 1
CONV_W = D_MODEL // 2
CONV_K = 31
CONV_PAD = CONV_K // 2
M_HEADS = 4
M_W = D_MODEL // 2
M_DH = M_W // M_HEADS
CHUNK = 128
N_MAIN = 2 * CONV_W + 4 * M_W
N_MEM = 256
X_HEADS = 4
X_DH = D_MODEL // X_HEADS
D_FF = 2816
FFN_K = 3
ALPHA = (2.0 * DEPTH) ** 0.25
LN_EPS = 1e-5

F32 = jnp.float32
BF16 = jnp.bfloat16

VMEM_LIMIT_BYTES = 56 * 1024 * 1024
LANES = 128
SUBLANES = 8
HALO_ROWS = 16
CONV_ROWS = 16
CONV_DELAY = 2

IN_TM = 512
SCAN_CHUNKS = 4
MIX_TM = 1024
MIX_ROWS = 128
FFN_TM = 512
FFN_ROWS = 256
FFN_SPLITS = 2


def _params(*sem):
    return pltpu.CompilerParams(dimension_semantics=sem, vmem_limit_bytes=VMEM_LIMIT_BYTES)


def _dot(a, b):
    return jnp.dot(a, b, preferred_element_type=F32)


def _dot_nt(a, b):
    return lax.dot_general(a, b, (((1,), (1,)), ((), ())), preferred_element_type=F32)


def _layer_norm(x, g, b):
    mu = jnp.mean(x, axis=-1, keepdims=True)
    xc = x - mu
    var = jnp.mean(xc * xc, axis=-1, keepdims=True)
    return xc * lax.rsqrt(var + LN_EPS) * g + b


def _const_spec(shape):
    return pl.BlockSpec(shape, lambda *_: (0,) * len(shape))


def _kv_kernel(mem_ref, wkt_ref, wv_ref, kt_ref, v_ref):
    mem = mem_ref[0].astype(BF16)
    kt_ref[0] = _dot_nt(wkt_ref[...], mem).astype(BF16)
    v_ref[0] = _dot(mem, wv_ref[...]).astype(BF16)


def _kv_proj(mem, wkt, wv):
    nb = mem.shape[0]
    return pl.pallas_call(
        _kv_kernel,
        grid=(nb,),
        in_specs=[pl.BlockSpec((1, N_MEM, D_MODEL), lambda b: (b, 0, 0)),
                  _const_spec((D_MODEL, D_MODEL)), _const_spec((D_MODEL, D_MODEL))],
        out_specs=[pl.BlockSpec((1, D_MODEL, N_MEM), lambda b: (b, 0, 0)),
                   pl.BlockSpec((1, N_MEM, D_MODEL), lambda b: (b, 0, 0))],
        out_shape=[jax.ShapeDtypeStruct((nb, D_MODEL, N_MEM), BF16),
                   jax.ShapeDtypeStruct((nb, N_MEM, D_MODEL), BF16)],
        compiler_params=_params("parallel"),
        name="kv_proj",
    )(mem, wkt, wv)


def _split_hi_lo(x):
    hi = x.astype(BF16)
    lo = (x - hi.astype(F32)).astype(BF16)
    return hi, lo


def _conv31_tasks(ucat_ref, ush_ref, cw_ref, cb_ref, out_ref, tm):
    n_sh = ush_ref.shape[1]

    def shifted_copies():
        x = ucat_ref[...]
        x = x.reshape(x.shape[0] // SUBLANES, SUBLANES, CONV_W)
        sub = lax.broadcasted_iota(jnp.int32, (1, SUBLANES, CONV_W), 1)
        for r in range(1, SUBLANES):
            rot = pltpu.roll(x, SUBLANES - r, 1)
            ush_ref[r - 1] = jnp.where(sub < SUBLANES - r, rot[:-1], rot[1:]).reshape(n_sh, CONV_W)

    groups = CONV_ROWS // SUBLANES

    def rows(r0):
        acc = jnp.zeros((groups, SUBLANES, CONV_W), F32) + cb_ref[...]
        for r in range(SUBLANES):
            taps = [(j, (HALO_ROWS - CONV_PAD + j) // SUBLANES) for j in range(CONV_K)
                    if (HALO_ROWS - CONV_PAD + j) % SUBLANES == r]
            a0, a1 = taps[0][1], taps[-1][1]
            src = ucat_ref if r == 0 else ush_ref.at[r - 1]
            blk = src[r0 + a0 * SUBLANES:r0 + a1 * SUBLANES + CONV_ROWS, :].reshape(a1 - a0 + groups, SUBLANES, CONV_W)
            for j, a in taps:
                acc = acc + blk[a - a0:a - a0 + groups] * cw_ref[j]
        out_ref[r0:r0 + CONV_ROWS, :] = acc.reshape(CONV_ROWS, CONV_W).astype(BF16)

    return [shifted_copies] + [functools.partial(rows, r0) for r0 in range(0, tm, CONV_ROWS)]


def _interleave(main, filler):
    done = 0
    for n, task in enumerate(main):
        task()
        upto = (n + 1) * len(filler) // len(main)
        for f in filler[done:upto]:
            f()
        done = upto


def _in_proj_kernel(x_ref, w_ref, b_ref, wt_ref, bt_ref, wg_ref, bg_ref, cw_ref, cb_ref,
                    uc_ref, k_ref, o_ref, qt_ref, vt_ref, cc_ref, br_ref, cr_ref, ucat_ref, ush_ref,
                    *, blocks_per_seq):
    i = pl.program_id(0)
    n_blocks = pl.num_programs(0) - CONV_DELAY
    tm = IN_TM
    slot_new = lax.rem(i, CONV_DELAY + 1)
    slot_prev = lax.rem(i + CONV_DELAY, CONV_DELAY + 1)
    slot_conv = lax.rem(i + 1, CONV_DELAY + 1)

    @pl.when(i == 0)
    def _():
        ucat_ref[...] = jnp.zeros_like(ucat_ref)

    xb = x_ref[...].astype(BF16)
    starts_seq = lax.rem(i, blocks_per_seq) == 0

    def seg(j):
        sl = slice(j * CONV_W, (j + 1) * CONV_W)
        return _dot(xb, w_ref[:, sl]) + b_ref[:, sl]

    def glu():
        u = seg(0) * jax.nn.sigmoid(seg(1))
        ucat_ref[slot_new, HALO_ROWS:HALO_ROWS + tm, :] = u
        ucat_ref[slot_new, 0:HALO_ROWS, :] = jnp.where(starts_seq, 0.0, ucat_ref[slot_prev, tm:tm + HALO_ROWS, :])
        ucat_ref[slot_prev, HALO_ROWS + tm:, :] = jnp.where(jnp.logical_or(starts_seq, i >= n_blocks), 0.0,
                                                            u[0:HALO_ROWS, :])

    def k_proj():
        k_ref[...] = (seg(2) * (M_DH ** -0.5)).astype(BF16)

    def o_proj():
        o_ref[...] = jax.nn.sigmoid(seg(3)).astype(BF16)

    def q_proj():
        qt_ref[...] = (_dot_nt(wt_ref[0:M_W, :], xb) + bt_ref[0:M_W, :]).astype(BF16)

    def v_proj():
        vt_ref[...] = (_dot_nt(wt_ref[M_W:, :], xb) + bt_ref[M_W:, :]).astype(BF16)

    def gates():
        g = _dot(xb, wg_ref[...]) + bg_ref[...]
        lf_c = jax.nn.log_sigmoid(g)
        gi_c = pltpu.roll(g, LANES - 2 * M_HEADS, 1)

        r = lax.broadcasted_iota(jnp.int32, (CHUNK, CHUNK), 0)
        c = lax.broadcasted_iota(jnp.int32, (CHUNK, CHUNK), 1)
        ge = jnp.where(r >= c, 1.0, 0.0).astype(BF16)
        le = jnp.where(r <= c, 1.0, 0.0).astype(BF16)
        fwd_lane = lax.broadcasted_iota(jnp.int32, (CHUNK, LANES), 1) < M_HEADS
        fwd_row = lax.broadcasted_iota(jnp.int32, (2 * SUBLANES, CHUNK), 0) < M_HEADS

        for ci in range(tm // CHUNK):
            sl = slice(ci * CHUNK, (ci + 1) * CHUNK)
            hi, lo = _split_hi_lo(lf_c[sl])
            b_c = jnp.where(fwd_lane, _dot(ge, hi) + _dot(ge, lo), _dot(le, hi) + _dot(le, lo))
            cc_ref[sl, :] = (b_c - gi_c[sl])[:, :2 * M_HEADS]
            g_r = g[sl].T[0:2 * SUBLANES, :]
            hi, lo = _split_hi_lo(jax.nn.log_sigmoid(g_r))
            b_r = jnp.where(fwd_row, _dot(hi, le) + _dot(lo, le), _dot(hi, ge) + _dot(lo, ge))[0:SUBLANES]
            br_ref[:, sl] = b_r
            cr_ref[:, sl] = b_r - g_r[SUBLANES:, :]

    conv = _conv31_tasks(ucat_ref.at[slot_conv], ush_ref, cw_ref, cb_ref, uc_ref, tm)
    conv[0]()
    _interleave([glu, k_proj, o_proj, q_proj, v_proj, gates], conv[1:])


def _in_proj(x, seq, w_tok, b_tok, w_fm, b_fm, w_gate, b_gate, conv_w):
    t = x.shape[0]
    tm = IN_TM
    nblk = t // tm
    cur = lambda i: jnp.minimum(i, nblk - 1)
    tok_spec = lambda n: pl.BlockSpec((tm, n), lambda i: (cur(i), 0))
    fm_spec = lambda n: pl.BlockSpec((n, tm), lambda i: (0, cur(i)))
    consts = (w_tok, b_tok, w_fm, b_fm, w_gate, b_gate) + tuple(conv_w)
    ucat_rows = tm + 2 * HALO_ROWS
    return pl.pallas_call(
        functools.partial(_in_proj_kernel, blocks_per_seq=seq // tm),
        grid=(nblk + CONV_DELAY,),
        in_specs=[tok_spec(D_MODEL)] + [_const_spec(a.shape) for a in consts],
        out_specs=[pl.BlockSpec((tm, CONV_W), lambda i: (jnp.maximum(i - CONV_DELAY, 0), 0))] + [tok_spec(CONV_W)] * 2
                  + [fm_spec(M_W)] * 2 + [tok_spec(2 * M_HEADS)] + [fm_spec(SUBLANES)] * 2,
        out_shape=[jax.ShapeDtypeStruct((t, CONV_W), BF16)] * 3 + [jax.ShapeDtypeStruct((M_W, t), BF16)] * 2
                  + [jax.ShapeDtypeStruct((t, 2 * M_HEADS), F32)] + [jax.ShapeDtypeStruct((SUBLANES, t), F32)] * 2,
        scratch_shapes=[pltpu.VMEM((CONV_DELAY + 1, ucat_rows, CONV_W), F32),
                        pltpu.VMEM((SUBLANES - 1, ucat_rows - SUBLANES, CONV_W), F32)],
        compiler_params=_params("arbitrary"),
        name="in_proj",
    )(x, *consts)


def _mlstm_kernel(kf_ref, qtf_ref, vtf_ref, ccf_ref, brf_ref, crf_ref,
                  kb_ref, qtb_ref, vtb_ref, ccb_ref, brb_ref, crb_ref,
                  hf_ref, hb_ref, cn_ref, m_ref):
    @pl.when(pl.program_id(1) == 0)
    def _():
        cn_ref[...] = jnp.zeros_like(cn_ref)
        m_ref[...] = jnp.zeros_like(m_ref)

    r = lax.broadcasted_iota(jnp.int32, (CHUNK, CHUNK), 0)
    c = lax.broadcasted_iota(jnp.int32, (CHUNK, CHUNK), 1)
    masks = (r <= c, r >= c)
    ones_rows = jnp.where(lax.broadcasted_iota(jnp.int32, (M_DH, CHUNK), 0) == 0, 1.0, 0.0).astype(BF16)
    refs = ((kf_ref, qtf_ref, vtf_ref, ccf_ref, brf_ref, crf_ref, hf_ref),
            (kb_ref, qtb_ref, vtb_ref, ccb_ref, brb_ref, crb_ref, hb_ref))
    def load_jobs(step):
        jobs = []
        for dirn in range(2):
            k_ref, qt_ref, vt_ref, cc_ref, br_ref, cr_ref, h_ref = refs[dirn]
            ci = step if dirn == 0 else SCAN_CHUNKS - 1 - step
            sl = slice(ci * CHUNK, (ci + 1) * CHUNK)
            end = CHUNK - 1 if dirn == 0 else 0
            for hd in range(M_HEADS):
                idx = dirn * M_HEADS + hd
                hs = slice(hd * M_DH, (hd + 1) * M_DH)
                b_row = br_ref[idx:idx + 1, sl]
                jobs.append(dict(
                    idx=idx, out=(h_ref, sl, hs), mask=masks[dirn], k=k_ref[sl, hs], qt=qt_ref[hs, sl],
                    vt_aug=jnp.concatenate([vt_ref[hs, sl], ones_rows], axis=0), c_col=cc_ref[sl, idx:idx + 1],
                    b_row=b_row, c_row=cr_ref[idx:idx + 1, sl], b_end=b_row[:, end:end + 1]))
        return jobs

    def carry_free(jobs):
        for j in jobs:
            j["qk"] = _dot(j["k"], j["qt"])
            w_log = j["b_end"] - j["c_row"]
            j["w_max"] = jnp.max(w_log, axis=1, keepdims=True)
            vw = (j["vt_aug"].astype(F32) * jnp.exp(w_log - j["w_max"])).astype(BF16)
            j["kv"] = _dot(vw, j["k"])

    def decay_weights(jobs):
        for j in jobs:
            m_prev = m_ref[j["idx"]][0:1, :]
            d = jnp.where(j["mask"], j["b_row"] - j["c_col"], -jnp.inf)
            a = j["b_row"] + m_prev
            m_t = jnp.maximum(a, jnp.max(d, axis=0, keepdims=True))
            j["m_t"] = m_t
            j["s"] = (j["qk"] * jnp.exp(d - m_t)).astype(BF16)
            j["q_carry"] = (j["qt"].astype(F32) * jnp.exp(a - m_t)).astype(BF16)
            j["a_end"] = j["b_end"] + m_prev

    def outputs_and_state(jobs):
        for j in jobs:
            cn_prev = cn_ref[j["idx"]]
            j["nd"] = _dot(jnp.concatenate([cn_prev.astype(BF16), j["vt_aug"]], axis=1),
                           jnp.concatenate([j["q_carry"], j["s"]], axis=0))
            m_new = jnp.maximum(j["a_end"], j["w_max"])
            cn_ref[j["idx"]] = jnp.exp(j["a_end"] - m_new) * cn_prev + jnp.exp(j["w_max"] - m_new) * j["kv"]
            m_ref[j["idx"]] = jnp.broadcast_to(m_new, m_ref.shape[1:])
        for j in jobs:
            nd = j["nd"]
            den = jnp.maximum(jnp.abs(nd[M_DH:M_DH + 1, :]), jnp.exp(-j["m_t"]))
            h_ref, sl, hs = j["out"]
            h_ref[sl, hs] = (nd[:M_DH, :] / den).T.astype(BF16)

    jobs = load_jobs(0)
    carry_free(jobs)
    for step in range(SCAN_CHUNKS):
        decay_weights(jobs)
        nxt = load_jobs(step + 1) if step + 1 < SCAN_CHUNKS else None
        if nxt is not None:
            carry_free(nxt)
        outputs_and_state(jobs)
        jobs = nxt


def _mlstm_scan(k, qt, vt, c_col, b_row, c_row, nb, seq):
    t = k.shape[0]
    bt = SCAN_CHUNKS * CHUNK
    n = seq // bt
    fwd = lambda b, i: (b * n + i, 0)
    bwd = lambda b, i: (b * n + n - 1 - i, 0)
    fwd_r = lambda b, i: (0, b * n + i)
    bwd_r = lambda b, i: (0, b * n + n - 1 - i)

    def specs(tok_map, fm_map):
        return [pl.BlockSpec((bt, M_W), tok_map)] + [pl.BlockSpec((M_W, bt), fm_map)] * 2 \
               + [pl.BlockSpec((bt, 2 * M_HEADS), tok_map)] + [pl.BlockSpec((SUBLANES, bt), fm_map)] * 2

    args = (k, qt, vt, c_col, b_row, c_row)
    return pl.pallas_call(
        _mlstm_kernel,
        grid=(nb, n),
        in_specs=specs(fwd, fwd_r) + specs(bwd, bwd_r),
        out_specs=[pl.BlockSpec((bt, M_W), fwd), pl.BlockSpec((bt, M_W), bwd)],
        out_shape=[jax.ShapeDtypeStruct((t, M_W), BF16)] * 2,
        scratch_shapes=[pltpu.VMEM((2 * M_HEADS, 2 * M_DH, M_DH), F32),
                        pltpu.VMEM((2 * M_HEADS, SUBLANES, LANES), F32)],
        compiler_params=_params("parallel", "arbitrary"),
        name="mlstm_scan",
    )(*args, *args)


def _mix_attn_kernel(x_ref, uc_ref, hf_ref, hb_ref, o_ref, cg_ref, cbeta_ref, wo1_ref, bo1_ref, g1_ref, b1_ref,
                     wq_ref, kt_ref, v_ref, wo2_ref, g2_ref, b2_ref, out_ref):
    head_cols = [slice(hd * X_DH, (hd + 1) * X_DH) for hd in range(X_HEADS)]

    def branches(j):
        sl = j["rows"]
        uc = jax.nn.silu(_layer_norm(uc_ref[sl, :].astype(F32), cg_ref[...], cbeta_ref[...]))
        hm = (hf_ref[sl, :].astype(F32) + hb_ref[sl, :].astype(F32)) * o_ref[sl, :].astype(F32)
        j["mixed"] = jnp.concatenate([uc.astype(BF16), hm.astype(BF16)], axis=1)

    def out_proj(j):
        j["mix"] = _dot(j["mixed"], wo1_ref[...]) + bo1_ref[...]

    def norm1(j):
        j["x1"] = _layer_norm(ALPHA * x_ref[j["rows"], :] + j["mix"], g1_ref[...], b1_ref[...])

    def q_proj(j):
        j["qx"] = _dot(j["x1"].astype(BF16), wq_ref[...]).astype(BF16)

    def scores(j):
        j["sc"] = [_dot(j["qx"][:, hs], kt_ref[0, hs, :]) * (X_DH ** -0.5) for hs in head_cols]

    def softmax(j):
        probs = []
        for sc in j["sc"]:
            e = jnp.exp(sc - jnp.max(sc, axis=-1, keepdims=True))
            probs.append((e / jnp.sum(e, axis=-1, keepdims=True)).astype(BF16))
        j["pr"] = probs

    def context(j):
        j["ctx"] = jnp.concatenate([_dot(pr, v_ref[0, :, hs]).astype(BF16) for pr, hs in zip(j["pr"], head_cols)],
                                   axis=1)

    def attn_out(j):
        j["att"] = _dot(j["ctx"], wo2_ref[...])

    def norm2(j):
        out_ref[j["rows"], :] = _layer_norm(ALPHA * j["x1"] + j["att"], g2_ref[...], b2_ref[...])

    groups = [dict(rows=slice(r, r + MIX_ROWS)) for r in range(0, MIX_TM, MIX_ROWS)]
    for stage in (branches, out_proj, norm1, q_proj, scores, softmax, context, attn_out, norm2):
        for j in groups:
            stage(j)


def _mix_attn(x, uc, hf, hb, o, kt, v, weights, nb, seq):
    t = x.shape[0]
    tm = MIX_TM
    n = seq // tm
    tok = lambda b, i: (b * n + i, 0)
    (cg, cbeta, wo1, bo1, g1, b1, wq, wo2, g2, b2) = weights
    cs = _const_spec
    return pl.pallas_call(
        _mix_attn_kernel,
        grid=(nb, n),
        in_specs=[pl.BlockSpec((tm, D_MODEL), tok)] + [pl.BlockSpec((tm, M_W), tok)] * 4
                 + [cs(cg.shape), cs(cbeta.shape), cs(wo1.shape), cs(bo1.shape), cs(g1.shape), cs(b1.shape), cs(wq.shape),
                    pl.BlockSpec((1, D_MODEL, N_MEM), lambda b, i: (b, 0, 0)),
                    pl.BlockSpec((1, N_MEM, D_MODEL), lambda b, i: (b, 0, 0)),
                    cs(wo2.shape), cs(g2.shape), cs(b2.shape)],
        out_specs=pl.BlockSpec((tm, D_MODEL), tok),
        out_shape=jax.ShapeDtypeStruct((t, D_MODEL), F32),
        compiler_params=_params("parallel", "parallel"),
        name="mix_attn",
    )(x, uc, hf, hb, o, cg, cbeta, wo1, bo1, g1, b1, wq, kt, v, wo2, g2, b2)


def _ffn_halo_kernel(x_ref, w_ref, b_ref, out_ref):
    out_ref[...] = _dot(x_ref[...].astype(BF16), w_ref[...]) + b_ref[...]


def _ffn_halo(rows, w_gate, b_gate):
    r = rows.shape[0]
    return pl.pallas_call(
        _ffn_halo_kernel,
        grid=(1,),
        in_specs=[_const_spec(rows.shape), _const_spec(w_gate.shape), _const_spec(b_gate.shape)],
        out_specs=_const_spec((r, D_FF)),
        out_shape=jax.ShapeDtypeStruct((r, D_FF), F32),
        compiler_params=_params("arbitrary"),
        name="ffn_halo",
    )(rows, w_gate, b_gate)


def _gelu_tanh_x2(x):
    c = float(np.float32(np.sqrt(2.0 / np.pi)))
    return x * (1.0 + jnp.tanh(x * (c + (c * 0.044715) * (x * x))))


def _conv_ffn_kernel(x_ref, halo_ref, wg_ref, bg_ref, wv_ref, bv_ref, cw_ref, cb_ref, wd_ref, bd_ref,
                     g3_ref, b3_ref, out_ref):
    i = pl.program_id(1)
    n = pl.num_programs(1)
    tm = FFN_TM
    halo = halo_ref[0]
    w = D_FF // FFN_SPLITS
    nrow = tm // FFN_ROWS
    row = lax.broadcasted_iota(jnp.int32, (SUBLANES, 1), 0)
    xs = [x_ref[g * FFN_ROWS:(g + 1) * FFN_ROWS, :] for g in range(nrow)]
    xbs = [x.astype(BF16) for x in xs]
    accs = [jnp.zeros((FFN_ROWS, D_MODEL), F32) + bd_ref[...] for _ in range(nrow)]
    for s in range(FFN_SPLITS):
        sl = slice(s * w, (s + 1) * w)
        gps = [_dot(xb, wg_ref[:, sl]) + bg_ref[:, sl] for xb in xbs]
        vals = [_dot(xb, wv_ref[:, sl]) + bv_ref[:, sl] for xb in xbs]
        acts = []
        for g in range(nrow):
            before = gps[g - 1][FFN_ROWS - 1:, :] if g > 0 else jnp.where(i > 0, halo[0:1, sl], 0.0)
            after = gps[g + 1][0:1, :] if g + 1 < nrow else jnp.where(i < n - 1, halo[1:2, sl], 0.0)
            g_prev = pltpu.roll(gps[g], 1, 0)
            g_prev = jnp.concatenate([jnp.where(row == 0, before, g_prev[0:SUBLANES]), g_prev[SUBLANES:]], axis=0)
            g_next = pltpu.roll(gps[g], FFN_ROWS - 1, 0)
            g_next = jnp.concatenate([g_next[:FFN_ROWS - SUBLANES],
                                      jnp.where(row == SUBLANES - 1, after, g_next[FFN_ROWS - SUBLANES:])], axis=0)
            gate = g_prev * cw_ref[0:1, sl] + gps[g] * cw_ref[1:2, sl] + g_next * cw_ref[2:3, sl] + cb_ref[:, sl]
            acts.append((_gelu_tanh_x2(gate) * vals[g]).astype(BF16))
        accs = [acc + _dot(act, wd_ref[sl, :]) for acc, act in zip(accs, acts)]
    for g in range(nrow):
        out_ref[g * FFN_ROWS:(g + 1) * FFN_ROWS, :] = _layer_norm(ALPHA * xs[g] + accs[g], g3_ref[...], b3_ref[...])


def _conv_ffn(x2, weights, nb, seq):
    t = x2.shape[0]
    tm = FFN_TM
    n = seq // tm
    (wg, bg, wv, bv, cw, cb, wd, bd, g3, b3) = weights
    xr = x2.reshape(nb, n, tm, D_MODEL)
    last = xr[:, :, tm - 1]
    first = xr[:, :, 0]
    before = jnp.concatenate([last[:, :1], last[:, :-1]], axis=1)
    after = jnp.concatenate([first[:, 1:], first[:, -1:]], axis=1)
    rows = jnp.stack([before, after], axis=2).reshape(nb * n * 2, D_MODEL)
    halo = _ffn_halo(rows, wg, bg).reshape(nb * n, 2, D_FF)

    tok = lambda b, i: (b * n + i, 0)
    cs = _const_spec
    return pl.pallas_call(
        _conv_ffn_kernel,
        grid=(nb, n),
        in_specs=[pl.BlockSpec((tm, D_MODEL), tok),
                  pl.BlockSpec((1, 2, D_FF), lambda b, i: (b * n + i, 0, 0)),
                  cs(wg.shape), cs(bg.shape), cs(wv.shape), cs(bv.shape), cs(cw.shape), cs(cb.shape),
                  cs(wd.shape), cs(bd.shape), cs(g3.shape), cs(b3.shape)],
        out_specs=pl.BlockSpec((tm, D_MODEL), tok),
        out_shape=jax.ShapeDtypeStruct((t, D_MODEL), F32),
        compiler_params=_params("parallel", "parallel"),
        name="conv_ffn",
    )(x2, halo, wg, bg, wv, bv, cw, cb, wd, bd, g3, b3)


def _pad_cols(w, n):
    return jnp.pad(w, ((0, 0), (0, n - w.shape[1])))


def _prep_layer(w_in, b_in, conv_w, conv_b, cln_g, cln_b, w_out, b_out, ln1_g, ln1_b, xq_w, xkv_w, xo_w,
                ln2_g, ln2_b, up_w, up_b, fconv_w, fconv_b, down_w, down_b, ln3_g, ln3_b):
    row = lambda a: a.reshape(1, -1)
    wg = w_in[:, N_MAIN:].reshape(D_MODEL, 4, M_HEADS)
    bg = b_in[N_MAIN:].reshape(4, M_HEADS)
    gate_order = (1, 3, 0, 2)
    w_gate = _pad_cols(jnp.concatenate([wg[:, j] for j in gate_order], axis=1), LANES).astype(BF16)
    b_gate = _pad_cols(row(jnp.concatenate([bg[j] for j in gate_order])), LANES)
    conv_in = (jnp.broadcast_to(conv_w[:, None, :], (CONV_K, SUBLANES, CONV_W)), row(conv_b))
    seg = lambda a, j: a[..., j * CONV_W:(j + 1) * CONV_W]
    tok_segs, fm_segs = (0, 1, 3, 5), (2, 4)
    in_w = (jnp.concatenate([seg(w_in, j) for j in tok_segs], axis=1).astype(BF16),
            row(jnp.concatenate([seg(b_in, j) for j in tok_segs])),
            jnp.concatenate([seg(w_in, j).T for j in fm_segs], axis=0).astype(BF16),
            jnp.concatenate([seg(b_in, j) for j in fm_segs]).reshape(-1, 1),
            w_gate, b_gate, conv_in)
    kv_w = (xkv_w[:, :D_MODEL].T.astype(BF16), xkv_w[:, D_MODEL:].astype(BF16))
    mix_w = (row(cln_g), row(cln_b), w_out.astype(BF16), row(b_out), row(ln1_g), row(ln1_b),
             xq_w.astype(BF16), xo_w.astype(BF16), row(ln2_g), row(ln2_b))
    ffn_w = (up_w[:, :D_FF].astype(BF16), row(up_b[:D_FF]), up_w[:, D_FF:].astype(BF16), row(up_b[D_FF:]),
             fconv_w, row(fconv_b), (0.5 * down_w).astype(BF16), row(down_b), row(ln3_g), row(ln3_b))
    return in_w, kv_w, mix_w, ffn_w


def _encoder_layer(x, mem, prepped):
    in_w, kv_w, mix_w, ffn_w = prepped
    nb, seq, _ = x.shape
    xf = x.reshape(nb * seq, D_MODEL)
    kt, v_mem = _kv_proj(mem, *kv_w)
    uc, k, o, qt, vt, c_col, b_row, c_row = _in_proj(xf, seq, *in_w)
    hf, hb = _mlstm_scan(k, qt, vt, c_col, b_row, c_row, nb, seq)
    x2 = _mix_attn(xf, uc, hf, hb, o, kt, v_mem, mix_w, nb, seq)
    x3 = _conv_ffn(x2, ffn_w, nb, seq)
    return x3.reshape(nb, seq, D_MODEL)


def kernel(x_prompt, x_sample, mem_prompt, mem_sample, w_in, b_in, conv_w, conv_b, cln_g, cln_b, w_out, b_out,
           ln1_g, ln1_b, xq_w, xkv_w, xo_w, ln2_g, ln2_b, up_w, up_b, fconv_w, fconv_b, down_w, down_b,
           ln3_g, ln3_b):
    layer_w = (w_in, b_in, conv_w, conv_b, cln_g, cln_b, w_out, b_out, ln1_g, ln1_b, xq_w, xkv_w, xo_w,
               ln2_g, ln2_b, up_w, up_b, fconv_w, fconv_b, down_w, down_b, ln3_g, ln3_b)
    y_prompt, y_sample = x_prompt, x_sample
    for l in range(DEPTH):
        prepped = _prep_layer(*[w[l] for w in layer_w])
        y_prompt = _encoder_layer(y_prompt, mem_prompt, prepped)
        y_sample = _encoder_layer(y_sample, mem_sample, prepped)
    return (y_prompt, y_sample)
```

```python
import functools

import numpy as np
import jax
import jax.numpy as jnp
from jax import lax
from jax.experimental import pallas as pl
from jax.experimental.pallas import tpu as pltpu

D_MODEL = 1024
DEPTH = 1
CONV_W = D_MODEL // 2
CONV_K = 31
CONV_PAD = CONV_K // 2
M_HEADS = 4
M_W = D_MODEL // 2
M_DH = M_W // M_HEADS
CHUNK = 128
N_MAIN = 2 * CONV_W + 4 * M_W
N_MEM = 256
X_HEADS = 4
X_DH = D_MODEL // X_HEADS
D_FF = 2816
FFN_K = 3
ALPHA = (2.0 * DEPTH) ** 0.25
LN_EPS = 1e-5

F32 = jnp.float32
BF16 = jnp.bfloat16

VMEM_LIMIT_BYTES = 56 * 1024 * 1024
LANES = 128
SUBLANES = 8
HALO_ROWS = 16
CONV_ROWS = 16
CONV_DELAY = 2

IN_TM = 512
SCAN_CHUNKS = 8
MIX_TM = 1024
MIX_ROWS = 128
FFN_TM = 1024
FFN_ROWS = 256
FFN_SPLITS = 2


def _params(*sem):
    return pltpu.CompilerParams(dimension_semantics=sem, vmem_limit_bytes=VMEM_LIMIT_BYTES)


def _dot(a, b):
    return jnp.dot(a, b, preferred_element_type=F32)


def _dot_nt(a, b):
    return lax.dot_general(a, b, (((1,), (1,)), ((), ())), preferred_element_type=F32)


def _layer_norm(x, g, b):
    mu = jnp.mean(x, axis=-1, keepdims=True)
    xc = x - mu
    var = jnp.mean(xc * xc, axis=-1, keepdims=True)
    return xc * lax.rsqrt(var + LN_EPS) * g + b


def _const_spec(shape):
    return pl.BlockSpec(shape, lambda *_: (0,) * len(shape))


def _resident_spec(shape):
    return pl.BlockSpec(shape, lambda *_: (0,) * len(shape), pipeline_mode=pl.Buffered(1))


def _kv_kernel(mem_ref, wkt_ref, wv_ref, kt_ref, v_ref):
    mem = mem_ref[0].astype(BF16)
    kt_ref[0] = _dot_nt(wkt_ref[...], mem).astype(BF16)
    v_ref[0] = _dot(mem, wv_ref[...]).astype(BF16)


def _kv_proj(mem, wkt, wv):
    nb = mem.shape[0]
    return pl.pallas_call(
        _kv_kernel,
        grid=(nb,),
        in_specs=[pl.BlockSpec((1, N_MEM, D_MODEL), lambda b: (b, 0, 0)),
                  _const_spec((D_MODEL, D_MODEL)), _const_spec((D_MODEL, D_MODEL))],
        out_specs=[pl.BlockSpec((1, D_MODEL, N_MEM), lambda b: (b, 0, 0)),
                   pl.BlockSpec((1, N_MEM, D_MODEL), lambda b: (b, 0, 0))],
        out_shape=[jax.ShapeDtypeStruct((nb, D_MODEL, N_MEM), BF16),
                   jax.ShapeDtypeStruct((nb, N_MEM, D_MODEL), BF16)],
        compiler_params=_params("parallel"),
        name="kv_proj",
    )(mem, wkt, wv)


def _split_hi_lo(x):
    hi = x.astype(BF16)
    lo = (x - hi.astype(F32)).astype(BF16)
    return hi, lo


def _conv31_tasks(ucat_ref, ush_ref, cw_ref, cb_ref, out_ref, tm):
    n_sh = ush_ref.shape[1]

    def shifted_copies():
        x = ucat_ref[...]
        x = x.reshape(x.shape[0] // SUBLANES, SUBLANES, CONV_W)
        sub = lax.broadcasted_iota(jnp.int32, (1, SUBLANES, CONV_W), 1)
        for r in range(1, SUBLANES):
            rot = pltpu.roll(x, SUBLANES - r, 1)
            ush_ref[r - 1] = jnp.where(sub < SUBLANES - r, rot[:-1], rot[1:]).reshape(n_sh, CONV_W)

    groups = CONV_ROWS // SUBLANES

    def rows(r0):
        acc = jnp.zeros((groups, SUBLANES, CONV_W), F32) + cb_ref[...]
        for r in range(SUBLANES):
            taps = [(j, (HALO_ROWS - CONV_PAD + j) // SUBLANES) for j in range(CONV_K)
                    if (HALO_ROWS - CONV_PAD + j) % SUBLANES == r]
            a0, a1 = taps[0][1], taps[-1][1]
            src = ucat_ref if r == 0 else ush_ref.at[r - 1]
            blk = src[r0 + a0 * SUBLANES:r0 + a1 * SUBLANES + CONV_ROWS, :].reshape(a1 - a0 + groups, SUBLANES, CONV_W)
            for j, a in taps:
                acc = acc + blk[a - a0:a - a0 + groups] * cw_ref[j]
        out_ref[r0:r0 + CONV_ROWS, :] = acc.reshape(CONV_ROWS, CONV_W).astype(BF16)

    return [shifted_copies] + [functools.partial(rows, r0) for r0 in range(0, tm, CONV_ROWS)]


def _interleave(main, filler):
    done = 0
    for n, task in enumerate(main):
        task()
        upto = (n + 1) * len(filler) // len(main)
        for f in filler[done:upto]:
            f()
        done = upto


def _in_proj_kernel(x_ref, w_ref, b_ref, wt_ref, bt_ref, wg_ref, bg_ref, cw_ref, cb_ref,
                    uc_ref, k_ref, o_ref, qt_ref, vt_ref, cc_ref, br_ref, cr_ref, ucat_ref, ush_ref,
                    *, blocks_per_seq):
    i = pl.program_id(0)
    n_blocks = pl.num_programs(0) - CONV_DELAY
    tm = IN_TM
    slot_new = lax.rem(i, CONV_DELAY + 1)
    slot_prev = lax.rem(i + CONV_DELAY, CONV_DELAY + 1)
    slot_conv = lax.rem(i + 1, CONV_DELAY + 1)

    @pl.when(i == 0)
    def _():
        ucat_ref[...] = jnp.zeros_like(ucat_ref)

    xb = x_ref[...].astype(BF16)
    starts_seq = lax.rem(i, blocks_per_seq) == 0

    def seg(j):
        sl = slice(j * CONV_W, (j + 1) * CONV_W)
        return _dot(xb, w_ref[:, sl]) + b_ref[:, sl]

    def glu():
        u = seg(0) * jax.nn.sigmoid(seg(1))
        ucat_ref[slot_new, HALO_ROWS:HALO_ROWS + tm, :] = u
        ucat_ref[slot_new, 0:HALO_ROWS, :] = jnp.where(starts_seq, 0.0, ucat_ref[slot_prev, tm:tm + HALO_ROWS, :])
        ucat_ref[slot_prev, HALO_ROWS + tm:, :] = jnp.where(jnp.logical_or(starts_seq, i >= n_blocks), 0.0,
                                                            u[0:HALO_ROWS, :])

    def k_proj():
        k_ref[...] = (seg(2) * (M_DH ** -0.5)).astype(BF16)

    def o_proj():
        o_ref[...] = jax.nn.sigmoid(seg(3)).astype(BF16)

    def q_proj():
        qt_ref[...] = (_dot_nt(wt_ref[0:M_W, :], xb) + bt_ref[0:M_W, :]).astype(BF16)

    def v_proj():
        vt_ref[...] = (_dot_nt(wt_ref[M_W:, :], xb) + bt_ref[M_W:, :]).astype(BF16)

    def gates():
        g = _dot(xb, wg_ref[...]) + bg_ref[...]
        lf_c = jax.nn.log_sigmoid(g)
        gi_c = pltpu.roll(g, LANES - 2 * M_HEADS, 1)

        r = lax.broadcasted_iota(jnp.int32, (CHUNK, CHUNK), 0)
        c = lax.broadcasted_iota(jnp.int32, (CHUNK, CHUNK), 1)
        ge = jnp.where(r >= c, 1.0, 0.0).astype(BF16)
        le = jnp.where(r <= c, 1.0, 0.0).astype(BF16)
        fwd_lane = lax.broadcasted_iota(jnp.int32, (CHUNK, LANES), 1) < M_HEADS
        fwd_row = lax.broadcasted_iota(jnp.int32, (2 * SUBLANES, CHUNK), 0) < M_HEADS

        for ci in range(tm // CHUNK):
            sl = slice(ci * CHUNK, (ci + 1) * CHUNK)
            hi, lo = _split_hi_lo(lf_c[sl])
            b_c = jnp.where(fwd_lane, _dot(ge, hi) + _dot(ge, lo), _dot(le, hi) + _dot(le, lo))
            cc_ref[sl, :] = (b_c - gi_c[sl])[:, :2 * M_HEADS]
            g_r = g[sl].T[0:2 * SUBLANES, :]
            hi, lo = _split_hi_lo(jax.nn.log_sigmoid(g_r))
            b_r = jnp.where(fwd_row, _dot(hi, le) + _dot(lo, le), _dot(hi, ge) + _dot(lo, ge))[0:SUBLANES]
            br_ref[:, sl] = b_r
            cr_ref[:, sl] = b_r - g_r[SUBLANES:, :]

    conv = _conv31_tasks(ucat_ref.at[slot_conv], ush_ref, cw_ref, cb_ref, uc_ref, tm)
    conv[0]()
    _interleave([glu, k_proj, o_proj, q_proj, v_proj, gates], conv[1:])


def _in_proj(x, seq, w_tok, b_tok, w_fm, b_fm, w_gate, b_gate, conv_w):
    t = x.shape[0]
    tm = IN_TM
    nblk = t // tm
    cur = lambda i: jnp.minimum(i, nblk - 1)
    tok_spec = lambda n: pl.BlockSpec((tm, n), lambda i: (cur(i), 0))
    fm_spec = lambda n: pl.BlockSpec((n, tm), lambda i: (0, cur(i)))
    consts = (w_tok, b_tok, w_fm, b_fm, w_gate, b_gate) + tuple(conv_w)
    ucat_rows = tm + 2 * HALO_ROWS
    return pl.pallas_call(
        functools.partial(_in_proj_kernel, blocks_per_seq=seq // tm),
        grid=(nblk + CONV_DELAY,),
        in_specs=[tok_spec(D_MODEL)] + [_const_spec(a.shape) for a in consts],
        out_specs=[pl.BlockSpec((tm, CONV_W), lambda i: (jnp.maximum(i - CONV_DELAY, 0), 0))] + [tok_spec(CONV_W)] * 2
                  + [fm_spec(M_W)] * 2 + [tok_spec(2 * M_HEADS)] + [fm_spec(SUBLANES)] * 2,
        out_shape=[jax.ShapeDtypeStruct((t, CONV_W), BF16)] * 3 + [jax.ShapeDtypeStruct((M_W, t), BF16)] * 2
                  + [jax.ShapeDtypeStruct((t, 2 * M_HEADS), F32)] + [jax.ShapeDtypeStruct((SUBLANES, t), F32)] * 2,
        scratch_shapes=[pltpu.VMEM((CONV_DELAY + 1, ucat_rows, CONV_W), F32),
                        pltpu.VMEM((SUBLANES - 1, ucat_rows - SUBLANES, CONV_W), F32)],
        compiler_params=_params("arbitrary"),
        name="in_proj",
    )(x, *consts)


def _mlstm_kernel(kf_ref, qtf_ref, vtf_ref, ccf_ref, brf_ref, crf_ref,
                  kb_ref, qtb_ref, vtb_ref, ccb_ref, brb_ref, crb_ref,
                  hf_ref, hb_ref, cn_ref, m_ref):
    @pl.when(pl.program_id(1) == 0)
    def _():
        cn_ref[...] = jnp.zeros_like(cn_ref)
        m_ref[...] = jnp.zeros_like(m_ref)

    r = lax.broadcasted_iota(jnp.int32, (CHUNK, CHUNK), 0)
    c = lax.broadcasted_iota(jnp.int32, (CHUNK, CHUNK), 1)
    masks = (r <= c, r >= c)
    ones_rows = jnp.where(lax.broadcasted_iota(jnp.int32, (M_DH, CHUNK), 0) == 0, 1.0, 0.0).astype(BF16)
    refs = ((kf_ref, qtf_ref, vtf_ref, ccf_ref, brf_ref, crf_ref, hf_ref),
            (kb_ref, qtb_ref, vtb_ref, ccb_ref, brb_ref, crb_ref, hb_ref))
    def load_jobs(step):
        jobs = []
        for dirn in range(2):
            k_ref, qt_ref, vt_ref, cc_ref, br_ref, cr_ref, h_ref = refs[dirn]
            ci = step if dirn == 0 else SCAN_CHUNKS - 1 - step
            sl = slice(ci * CHUNK, (ci + 1) * CHUNK)
            end = CHUNK - 1 if dirn == 0 else 0
            for hd in range(M_HEADS):
                idx = dirn * M_HEADS + hd
                hs = slice(hd * M_DH, (hd + 1) * M_DH)
                b_row = br_ref[idx:idx + 1, sl]
                jobs.append(dict(
                    idx=idx, out=(h_ref, sl, hs), mask=masks[dirn], k=k_ref[sl, hs], qt=qt_ref[hs, sl],
                    vt_aug=jnp.concatenate([vt_ref[hs, sl], ones_rows], axis=0), c_col=cc_ref[sl, idx:idx + 1],
                    b_row=b_row, c_row=cr_ref[idx:idx + 1, sl], b_end=b_row[:, end:end + 1]))
        return jobs

    def carry_free(jobs):
        for j in jobs:
            j["qk"] = _dot(j["k"], j["qt"])
            w_log = j["b_end"] - j["c_row"]
            j["w_max"] = jnp.max(w_log, axis=1, keepdims=True)
            vw = j["vt_aug"] * jnp.exp(w_log - j["w_max"]).astype(BF16)
            j["kv"] = _dot(vw, j["k"])

    def decay_weights(jobs):
        for j in jobs:
            m_prev = m_ref[j["idx"]][0:1, :]
            d = jnp.where(j["mask"], j["b_row"] - j["c_col"], -jnp.inf)
            a = j["b_row"] + m_prev
            m_t = jnp.maximum(a, jnp.max(d, axis=0, keepdims=True))
            j["m_t"] = m_t
            j["s"] = (j["qk"] * jnp.exp(d - m_t)).astype(BF16)
            j["q_carry"] = j["qt"] * jnp.exp(a - m_t).astype(BF16)
            j["a_end"] = j["b_end"] + m_prev

    def outputs_and_state(jobs):
        for j in jobs:
            cn_prev = cn_ref[j["idx"]]
            j["nd"] = _dot(jnp.concatenate([cn_prev.astype(BF16), j["vt_aug"]], axis=1),
                           jnp.concatenate([j["q_carry"], j["s"]], axis=0))
            m_new = jnp.maximum(j["a_end"], j["w_max"])
            cn_ref[j["idx"]] = jnp.exp(j["a_end"] - m_new) * cn_prev + jnp.exp(j["w_max"] - m_new) * j["kv"]
            m_ref[j["idx"]] = jnp.broadcast_to(m_new, m_ref.shape[1:])
        for j in jobs:
            nd = j["nd"]
            den = jnp.maximum(jnp.abs(nd[M_DH:M_DH + 1, :]), jnp.exp(-j["m_t"]))
            h_ref, sl, hs = j["out"]
            h_ref[sl, hs] = (nd[:M_DH, :] / den).T.astype(BF16)

    jobs = load_jobs(0)
    carry_free(jobs)
    for step in range(SCAN_CHUNKS):
        nxt = load_jobs(step + 1) if step + 1 < SCAN_CHUNKS else None
        for n in range(len(jobs)):
            decay_weights(jobs[n:n + 1])
            if nxt is not None:
                carry_free(nxt[n:n + 1])
        outputs_and_state(jobs)
        jobs = nxt


def _mlstm_scan(k, qt, vt, c_col, b_row, c_row, nb, seq):
    t = k.shape[0]
    bt = SCAN_CHUNKS * CHUNK
    n = seq // bt
    fwd = lambda b, i: (b * n + i, 0)
    bwd = lambda b, i: (b * n + n - 1 - i, 0)
    fwd_r = lambda b, i: (0, b * n + i)
    bwd_r = lambda b, i: (0, b * n + n - 1 - i)

    def specs(tok_map, fm_map):
        return [pl.BlockSpec((bt, M_W), tok_map)] + [pl.BlockSpec((M_W, bt), fm_map)] * 2 \
               + [pl.BlockSpec((bt, 2 * M_HEADS), tok_map)] + [pl.BlockSpec((SUBLANES, bt), fm_map)] * 2

    args = (k, qt, vt, c_col, b_row, c_row)
    return pl.pallas_call(
        _mlstm_kernel,
        grid=(nb, n),
        in_specs=specs(fwd, fwd_r) + specs(bwd, bwd_r),
        out_specs=[pl.BlockSpec((bt, M_W), fwd), pl.BlockSpec((bt, M_W), bwd)],
        out_shape=[jax.ShapeDtypeStruct((t, M_W), BF16)] * 2,
        scratch_shapes=[pltpu.VMEM((2 * M_HEADS, 2 * M_DH, M_DH), F32),
                        pltpu.VMEM((2 * M_HEADS, SUBLANES, LANES), F32)],
        compiler_params=_params("parallel", "arbitrary"),
        name="mlstm_scan",
    )(*args, *args)


def _mix_attn_kernel(x_ref, uc_ref, hf_ref, hb_ref, o_ref, cg_ref, cbeta_ref, wo1_ref, bo1_ref, g1_ref, b1_ref,
                     wq_ref, kt_ref, v_ref, wo2_ref, g2_ref, b2_ref, out_ref):
    head_cols = [slice(hd * X_DH, (hd + 1) * X_DH) for hd in range(X_HEADS)]

    def branches(j):
        sl = j["rows"]
        uc = jax.nn.silu(_layer_norm(uc_ref[sl, :].astype(F32), cg_ref[...], cbeta_ref[...]))
        hm = (hf_ref[sl, :].astype(F32) + hb_ref[sl, :].astype(F32)) * o_ref[sl, :].astype(F32)
        j["mixed"] = jnp.concatenate([uc.astype(BF16), hm.astype(BF16)], axis=1)

    def out_proj(j):
        j["mix"] = _dot(j["mixed"], wo1_ref[...]) + bo1_ref[...]

    def norm1(j):
        j["x1"] = _layer_norm(ALPHA * x_ref[j["rows"], :] + j["mix"], g1_ref[...], b1_ref[...])

    def q_proj(j):
        j["qx"] = _dot(j["x1"].astype(BF16), wq_ref[...]).astype(BF16)

    def scores(j):
        j["sc"] = [_dot(j["qx"][:, hs], kt_ref[0, hs, :]) * (X_DH ** -0.5) for hs in head_cols]

    def softmax(j):
        probs = []
        for sc in j["sc"]:
            e = jnp.exp(sc - jnp.max(sc, axis=-1, keepdims=True))
            probs.append((e / jnp.sum(e, axis=-1, keepdims=True)).astype(BF16))
        j["pr"] = probs

    def context(j):
        j["ctx"] = jnp.concatenate([_dot(pr, v_ref[0, :, hs]).astype(BF16) for pr, hs in zip(j["pr"], head_cols)],
                                   axis=1)

    def attn_out(j):
        j["att"] = _dot(j["ctx"], wo2_ref[...])

    def norm2(j):
        out_ref[j["rows"], :] = _layer_norm(ALPHA * j["x1"] + j["att"], g2_ref[...], b2_ref[...])

    groups = [dict(rows=slice(r, r + MIX_ROWS)) for r in range(0, MIX_TM, MIX_ROWS)]
    for stage in (branches, out_proj, norm1, q_proj, scores, softmax, context, attn_out, norm2):
        for j in groups:
            stage(j)


def _mix_attn(x, uc, hf, hb, o, kt, v, weights, nb, seq):
    t = x.shape[0]
    tm = MIX_TM
    n = seq // tm
    tok = lambda b, i: (b * n + i, 0)
    (cg, cbeta, wo1, bo1, g1, b1, wq, wo2, g2, b2) = weights
    cs = _const_spec
    return pl.pallas_call(
        _mix_attn_kernel,
        grid=(nb, n),
        in_specs=[pl.BlockSpec((tm, D_MODEL), tok)] + [pl.BlockSpec((tm, M_W), tok)] * 4
                 + [cs(cg.shape), cs(cbeta.shape), cs(wo1.shape), cs(bo1.shape), cs(g1.shape), cs(b1.shape), cs(wq.shape),
                    pl.BlockSpec((1, D_MODEL, N_MEM), lambda b, i: (b, 0, 0)),
                    pl.BlockSpec((1, N_MEM, D_MODEL), lambda b, i: (b, 0, 0)),
                    cs(wo2.shape), cs(g2.shape), cs(b2.shape)],
        out_specs=pl.BlockSpec((tm, D_MODEL), tok),
        out_shape=jax.ShapeDtypeStruct((t, D_MODEL), F32),
        compiler_params=_params("parallel", "parallel"),
        name="mix_attn",
    )(x, uc, hf, hb, o, cg, cbeta, wo1, bo1, g1, b1, wq, kt, v, wo2, g2, b2)


def _ffn_halo_kernel(x_ref, w_ref, b_ref, out_ref):
    out_ref[...] = _dot(x_ref[...].astype(BF16), w_ref[...]) + b_ref[...]


def _ffn_halo(rows, w_gate, b_gate):
    r = rows.shape[0]
    return pl.pallas_call(
        _ffn_halo_kernel,
        grid=(1,),
        in_specs=[_const_spec(rows.shape), _const_spec(w_gate.shape), _const_spec(b_gate.shape)],
        out_specs=_const_spec((r, D_FF)),
        out_shape=jax.ShapeDtypeStruct((r, D_FF), F32),
        compiler_params=_params("arbitrary"),
        name="ffn_halo",
    )(rows, w_gate, b_gate)


def _gelu_tanh_x2(x):
    c = float(np.float32(np.sqrt(2.0 / np.pi)))
    return x * (1.0 + jnp.tanh(x * (c + (c * 0.044715) * (x * x))))


def _conv_ffn_kernel(x_ref, halo_ref, wg_ref, bg_ref, wv_ref, bv_ref, cw_ref, cb_ref, wd_ref, bd_ref,
                     g3_ref, b3_ref, out_ref):
    i = pl.program_id(1)
    n = pl.num_programs(1)
    tm = FFN_TM
    halo = halo_ref[0]
    w = D_FF // FFN_SPLITS
    nrow = tm // FFN_ROWS
    row = lax.broadcasted_iota(jnp.int32, (SUBLANES, 1), 0)
    xs = [x_ref[g * FFN_ROWS:(g + 1) * FFN_ROWS, :] for g in range(nrow)]
    xbs = [x.astype(BF16) for x in xs]
    accs = [jnp.zeros((FFN_ROWS, D_MODEL), F32) + bd_ref[...] for _ in range(nrow)]
    for s in range(FFN_SPLITS):
        sl = slice(s * w, (s + 1) * w)
        gps = [_dot(xb, wg_ref[:, sl]) + bg_ref[:, sl] for xb in xbs]
        vals = [_dot(xb, wv_ref[:, sl]) + bv_ref[:, sl] for xb in xbs]
        acts = []
        for g in range(nrow):
            before = gps[g - 1][FFN_ROWS - 1:, :] if g > 0 else jnp.where(i > 0, halo[0:1, sl], 0.0)
            after = gps[g + 1][0:1, :] if g + 1 < nrow else jnp.where(i < n - 1, halo[1:2, sl], 0.0)
            g_prev = pltpu.roll(gps[g], 1, 0)
            g_prev = jnp.concatenate([jnp.where(row == 0, before, g_prev[0:SUBLANES]), g_prev[SUBLANES:]], axis=0)
            g_next = pltpu.roll(gps[g], FFN_ROWS - 1, 0)
            g_next = jnp.concatenate([g_next[:FFN_ROWS - SUBLANES],
                                      jnp.where(row == SUBLANES - 1, after, g_next[FFN_ROWS - SUBLANES:])], axis=0)
            gate = g_prev * cw_ref[0:1, sl] + gps[g] * cw_ref[1:2, sl] + g_next * cw_ref[2:3, sl] + cb_ref[:, sl]
            acts.append((_gelu_tanh_x2(gate) * vals[g]).astype(BF16))
        accs = [acc + _dot(act, wd_ref[sl, :]) for acc, act in zip(accs, acts)]
    for g in range(nrow):
        out_ref[g * FFN_ROWS:(g + 1) * FFN_ROWS, :] = _layer_norm(ALPHA * xs[g] + accs[g], g3_ref[...], b3_ref[...])


def _conv_ffn(x2, weights, nb, seq):
    t = x2.shape[0]
    tm = FFN_TM
    n = seq // tm
    (wg, bg, wv, bv, cw, cb, wd, bd, g3, b3) = weights
    xr = x2.reshape(nb, n, tm, D_MODEL)
    last = xr[:, :, tm - 1]
    first = xr[:, :, 0]
    before = jnp.concatenate([last[:, :1], last[:, :-1]], axis=1)
    after = jnp.concatenate([first[:, 1:], first[:, -1:]], axis=1)
    rows = jnp.stack([before, after], axis=2).reshape(nb * n * 2, D_MODEL)
    halo = _ffn_halo(rows, wg, bg).reshape(nb * n, 2, D_FF)

    tok = lambda b, i: (b * n + i, 0)
    cs = _const_spec
    return pl.pallas_call(
        _conv_ffn_kernel,
        grid=(nb, n),
        in_specs=[pl.BlockSpec((tm, D_MODEL), tok),
                  pl.BlockSpec((1, 2, D_FF), lambda b, i: (b * n + i, 0, 0)),
                  _resident_spec(wg.shape), cs(bg.shape), _resident_spec(wv.shape), cs(bv.shape), cs(cw.shape),
                  cs(cb.shape), _resident_spec(wd.shape), cs(bd.shape), cs(g3.shape), cs(b3.shape)],
        out_specs=pl.BlockSpec((tm, D_MODEL), tok),
        out_shape=jax.ShapeDtypeStruct((t, D_MODEL), F32),
        compiler_params=_params("parallel", "parallel"),
        name="conv_ffn",
    )(x2, halo, wg, bg, wv, bv, cw, cb, wd, bd, g3, b3)


def _pad_cols(w, n):
    return jnp.pad(w, ((0, 0), (0, n - w.shape[1])))


def _prep_layer(w_in, b_in, conv_w, conv_b, cln_g, cln_b, w_out, b_out, ln1_g, ln1_b, xq_w, xkv_w, xo_w,
                ln2_g, ln2_b, up_w, up_b, fconv_w, fconv_b, down_w, down_b, ln3_g, ln3_b):
    row = lambda a: a.reshape(1, -1)
    wg = w_in[:, N_MAIN:].reshape(D_MODEL, 4, M_HEADS)
    bg = b_in[N_MAIN:].reshape(4, M_HEADS)
    gate_order = (1, 3, 0, 2)
    w_gate = _pad_cols(jnp.concatenate([wg[:, j] for j in gate_order], axis=1), LANES).astype(BF16)
    b_gate = _pad_cols(row(jnp.concatenate([bg[j] for j in gate_order])), LANES)
    conv_in = (jnp.broadcast_to(conv_w[:, None, :], (CONV_K, SUBLANES, CONV_W)), row(conv_b))
    seg = lambda a, j: a[..., j * CONV_W:(j + 1) * CONV_W]
    tok_segs, fm_segs = (0, 1, 3, 5), (2, 4)
    in_w = (jnp.concatenate([seg(w_in, j) for j in tok_segs], axis=1).astype(BF16),
            row(jnp.concatenate([seg(b_in, j) for j in tok_segs])),
            jnp.concatenate([seg(w_in, j).T for j in fm_segs], axis=0).astype(BF16),
            jnp.concatenate([seg(b_in, j) for j in fm_segs]).reshape(-1, 1),
            w_gate, b_gate, conv_in)
    kv_w = (xkv_w[:, :D_MODEL].T.astype(BF16), xkv_w[:, D_MODEL:].astype(BF16))
    mix_w = (row(cln_g), row(cln_b), w_out.astype(BF16), row(b_out), row(ln1_g), row(ln1_b),
             xq_w.astype(BF16), xo_w.astype(BF16), row(ln2_g), row(ln2_b))
    ffn_w = (up_w[:, :D_FF].astype(BF16), row(up_b[:D_FF]), up_w[:, D_FF:].astype(BF16), row(up_b[D_FF:]),
             fconv_w, row(fconv_b), (0.5 * down_w).astype(BF16), row(down_b), row(ln3_g), row(ln3_b))
    return in_w, kv_w, mix_w, ffn_w


def _encoder_layer(x, mem, prepped):
    in_w, kv_w, mix_w, ffn_w = prepped
    nb, seq, _ = x.shape
    xf = x.reshape(nb * seq, D_MODEL)
    kt, v_mem = _kv_proj(mem, *kv_w)
    uc, k, o, qt, vt, c_col, b_row, c_row = _in_proj(xf, seq, *in_w)
    hf, hb = _mlstm_scan(k, qt, vt, c_col, b_row, c_row, nb, seq)
    x2 = _mix_attn(xf, uc, hf, hb, o, kt, v_mem, mix_w, nb, seq)
    x3 = _conv_ffn(x2, ffn_w, nb, seq)
    return x3.reshape(nb, seq, D_MODEL)


def kernel(x_prompt, x_sample, mem_prompt, mem_sample, w_in, b_in, conv_w, conv_b, cln_g, cln_b, w_out, b_out,
           ln1_g, ln1_b, xq_w, xkv_w, xo_w, ln2_g, ln2_b, up_w, up_b, fconv_w, fconv_b, down_w, down_b,
           ln3_g, ln3_b):
    layer_w = (w_in, b_in, conv_w, conv_b, cln_g, cln_b, w_out, b_out, ln1_g, ln1_b, xq_w, xkv_w, xo_w,
               ln2_g, ln2_b, up_w, up_b, fconv_w, fconv_b, down_w, down_b, ln3_g, ln3_b)
    y_prompt, y_sample = x_prompt, x_sample
    for l in range(DEPTH):
        prepped = _prep_layer(*[w[l] for w in layer_w])
        y_prompt = _encoder_layer(y_prompt, mem_prompt, prepped)
        y_sample = _encoder_layer(y_sample, mem_sample, prepped)
    return (y_prompt, y_sample)
```

```python
import functools

import numpy as np
import jax
import jax.numpy as jnp
from jax import lax
from jax.experimental import pallas as pl
from jax.experimental.pallas import tpu as pltpu

D_MODEL = 1024
DEPTH = 1
CONV_W = D_MODEL // 2
CONV_K = 31
CONV_PAD = CONV_K // 2
M_HEADS = 4
M_W = D_MODEL // 2
M_DH = M_W // M_HEADS
CHUNK = 128
N_MAIN = 2 * CONV_W + 4 * M_W
N_MEM = 256
X_HEADS = 4
X_DH = D_MODEL // X_HEADS
D_FF = 2816
FFN_K = 3
ALPHA = (2.0 * DEPTH) ** 0.25
LN_EPS = 1e-5

F32 = jnp.float32
BF16 = jnp.bfloat16

VMEM_LIMIT_BYTES = 56 * 1024 * 1024
LANES = 128
SUBLANES = 8
HALO_ROWS = 16
CONV_ROWS = 16
CONV_DELAY = 2

IN_TM = 512
SCAN_CHUNKS = 16
MIX_TM = 1024
MIX_ROWS = 128
FFN_TM = 1024
FFN_ROWS = 256
FFN_SPLITS = 2


def _params(*sem):
    return pltpu.CompilerParams(dimension_semantics=sem, vmem_limit_bytes=VMEM_LIMIT_BYTES)


def _dot(a, b):
    return jnp.dot(a, b, preferred_element_type=F32)


def _dot_nt(a, b):
    return lax.dot_general(a, b, (((1,), (1,)), ((), ())), preferred_element_type=F32)


def _layer_norm(x, g, b):
    mu = jnp.mean(x, axis=-1, keepdims=True)
    xc = x - mu
    var = jnp.mean(xc * xc, axis=-1, keepdims=True)
    return xc * lax.rsqrt(var + LN_EPS) * g + b


def _const_spec(shape):
    return pl.BlockSpec(shape, lambda *_: (0,) * len(shape))


def _resident_spec(shape):
    return pl.BlockSpec(shape, lambda *_: (0,) * len(shape), pipeline_mode=pl.Buffered(1))


def _kv_kernel(mem_ref, wkt_ref, wv_ref, kt_ref, v_ref):
    mem = mem_ref[0].astype(BF16)
    kt_ref[0] = _dot_nt(wkt_ref[...], mem).astype(BF16)
    v_ref[0] = _dot(mem, wv_ref[...]).astype(BF16)


def _kv_proj(mem, wkt, wv):
    nb = mem.shape[0]
    return pl.pallas_call(
        _kv_kernel,
        grid=(nb,),
        in_specs=[pl.BlockSpec((1, N_MEM, D_MODEL), lambda b: (b, 0, 0)),
                  _const_spec((D_MODEL, D_MODEL)), _const_spec((D_MODEL, D_MODEL))],
        out_specs=[pl.BlockSpec((1, D_MODEL, N_MEM), lambda b: (b, 0, 0)),
                   pl.BlockSpec((1, N_MEM, D_MODEL), lambda b: (b, 0, 0))],
        out_shape=[jax.ShapeDtypeStruct((nb, D_MODEL, N_MEM), BF16),
                   jax.ShapeDtypeStruct((nb, N_MEM, D_MODEL), BF16)],
        compiler_params=_params("parallel"),
        name="kv_proj",
    )(mem, wkt, wv)


def _split_hi_lo(x):
    hi = x.astype(BF16)
    lo = (x - hi.astype(F32)).astype(BF16)
    return hi, lo


def _conv31_tasks(ucat_ref, ush_ref, cw_ref, cb_ref, out_ref, tm):
    n_sh = ush_ref.shape[1]

    def shifted_copies():
        x = ucat_ref[...]
        x = x.reshape(x.shape[0] // SUBLANES, SUBLANES, CONV_W)
        sub = lax.broadcasted_iota(jnp.int32, (1, SUBLANES, CONV_W), 1)
        for r in range(1, SUBLANES):
            rot = pltpu.roll(x, SUBLANES - r, 1)
            ush_ref[r - 1] = jnp.where(sub < SUBLANES - r, rot[:-1], rot[1:]).reshape(n_sh, CONV_W)

    groups = CONV_ROWS // SUBLANES

    def rows(r0):
        acc = jnp.zeros((groups, SUBLANES, CONV_W), F32) + cb_ref[...]
        for r in range(SUBLANES):
            taps = [(j, (HALO_ROWS - CONV_PAD + j) // SUBLANES) for j in range(CONV_K)
                    if (HALO_ROWS - CONV_PAD + j) % SUBLANES == r]
            a0, a1 = taps[0][1], taps[-1][1]
            src = ucat_ref if r == 0 else ush_ref.at[r - 1]
            blk = src[r0 + a0 * SUBLANES:r0 + a1 * SUBLANES + CONV_ROWS, :].reshape(a1 - a0 + groups, SUBLANES, CONV_W)
            for j, a in taps:
                acc = acc + blk[a - a0:a - a0 + groups] * cw_ref[j]
        out_ref[r0:r0 + CONV_ROWS, :] = acc.reshape(CONV_ROWS, CONV_W).astype(BF16)

    return [shifted_copies] + [functools.partial(rows, r0) for r0 in range(0, tm, CONV_ROWS)]


def _interleave(main, filler):
    done = 0
    for n, task in enumerate(main):
        task()
        upto = (n + 1) * len(filler) // len(main)
        for f in filler[done:upto]:
            f()
        done = upto


def _in_proj_kernel(x_ref, w_ref, b_ref, wt_ref, bt_ref, wg_ref, bg_ref, cw_ref, cb_ref,
                    uc_ref, k_ref, o_ref, qt_ref, vt_ref, cc_ref, br_ref, cr_ref, ucat_ref, ush_ref,
                    *, blocks_per_seq):
    i = pl.program_id(0)
    n_blocks = pl.num_programs(0) - CONV_DELAY
    tm = IN_TM
    slot_new = lax.rem(i, CONV_DELAY + 1)
    slot_prev = lax.rem(i + CONV_DELAY, CONV_DELAY + 1)
    slot_conv = lax.rem(i + 1, CONV_DELAY + 1)

    @pl.when(i == 0)
    def _():
        ucat_ref[...] = jnp.zeros_like(ucat_ref)

    xb = x_ref[...].astype(BF16)
    starts_seq = lax.rem(i, blocks_per_seq) == 0

    def seg(j):
        sl = slice(j * CONV_W, (j + 1) * CONV_W)
        return _dot(xb, w_ref[:, sl]) + b_ref[:, sl]

    def glu():
        u = seg(0) * jax.nn.sigmoid(seg(1))
        ucat_ref[slot_new, HALO_ROWS:HALO_ROWS + tm, :] = u
        ucat_ref[slot_new, 0:HALO_ROWS, :] = jnp.where(starts_seq, 0.0, ucat_ref[slot_prev, tm:tm + HALO_ROWS, :])
        ucat_ref[slot_prev, HALO_ROWS + tm:, :] = jnp.where(jnp.logical_or(starts_seq, i >= n_blocks), 0.0,
                                                            u[0:HALO_ROWS, :])

    def k_proj():
        k_ref[...] = (seg(2) * (M_DH ** -0.5)).astype(BF16)

    def o_proj():
        o_ref[...] = jax.nn.sigmoid(seg(3)).astype(BF16)

    def q_proj():
        qt_ref[...] = (_dot_nt(wt_ref[0:M_W, :], xb) + bt_ref[0:M_W, :]).astype(BF16)

    def v_proj():
        vt_ref[...] = (_dot_nt(wt_ref[M_W:, :], xb) + bt_ref[M_W:, :]).astype(BF16)

    def gates():
        g = _dot(xb, wg_ref[...]) + bg_ref[...]
        lf_c = jax.nn.log_sigmoid(g)
        gi_c = pltpu.roll(g, LANES - 2 * M_HEADS, 1)

        r = lax.broadcasted_iota(jnp.int32, (CHUNK, CHUNK), 0)
        c = lax.broadcasted_iota(jnp.int32, (CHUNK, CHUNK), 1)
        ge = jnp.where(r >= c, 1.0, 0.0).astype(BF16)
        le = jnp.where(r <= c, 1.0, 0.0).astype(BF16)
        fwd_lane = lax.broadcasted_iota(jnp.int32, (CHUNK, LANES), 1) < M_HEADS
        fwd_row = lax.broadcasted_iota(jnp.int32, (2 * SUBLANES, CHUNK), 0) < M_HEADS

        for ci in range(tm // CHUNK):
            sl = slice(ci * CHUNK, (ci + 1) * CHUNK)
            hi, lo = _split_hi_lo(lf_c[sl])
            b_c = jnp.where(fwd_lane, _dot(ge, hi) + _dot(ge, lo), _dot(le, hi) + _dot(le, lo))
            cc_ref[sl, :] = (b_c - gi_c[sl])[:, :2 * M_HEADS]
            g_r = g[sl].T[0:2 * SUBLANES, :]
            hi, lo = _split_hi_lo(jax.nn.log_sigmoid(g_r))
            b_r = jnp.where(fwd_row, _dot(hi, le) + _dot(lo, le), _dot(hi, ge) + _dot(lo, ge))[0:SUBLANES]
            br_ref[:, sl] = b_r
            cr_ref[:, sl] = b_r - g_r[SUBLANES:, :]

    conv = _conv31_tasks(ucat_ref.at[slot_conv], ush_ref, cw_ref, cb_ref, uc_ref, tm)
    conv[0]()
    _interleave([glu, k_proj, o_proj, q_proj, v_proj, gates], conv[1:])


def _in_proj(x, seq, w_tok, b_tok, w_fm, b_fm, w_gate, b_gate, conv_w):
    t = x.shape[0]
    tm = IN_TM
    nblk = t // tm
    cur = lambda i: jnp.minimum(i, nblk - 1)
    tok_spec = lambda n: pl.BlockSpec((tm, n), lambda i: (cur(i), 0))
    fm_spec = lambda n: pl.BlockSpec((n, tm), lambda i: (0, cur(i)))
    consts = (w_tok, b_tok, w_fm, b_fm, w_gate, b_gate) + tuple(conv_w)
    ucat_rows = tm + 2 * HALO_ROWS
    return pl.pallas_call(
        functools.partial(_in_proj_kernel, blocks_per_seq=seq // tm),
        grid=(nblk + CONV_DELAY,),
        in_specs=[tok_spec(D_MODEL)] + [_const_spec(a.shape) for a in consts],
        out_specs=[pl.BlockSpec((tm, CONV_W), lambda i: (jnp.maximum(i - CONV_DELAY, 0), 0))] + [tok_spec(CONV_W)] * 2
                  + [fm_spec(M_W)] * 2 + [tok_spec(2 * M_HEADS)] + [fm_spec(SUBLANES)] * 2,
        out_shape=[jax.ShapeDtypeStruct((t, CONV_W), BF16)] * 3 + [jax.ShapeDtypeStruct((M_W, t), BF16)] * 2
                  + [jax.ShapeDtypeStruct((t, 2 * M_HEADS), F32)] + [jax.ShapeDtypeStruct((SUBLANES, t), F32)] * 2,
        scratch_shapes=[pltpu.VMEM((CONV_DELAY + 1, ucat_rows, CONV_W), F32),
                        pltpu.VMEM((SUBLANES - 1, ucat_rows - SUBLANES, CONV_W), F32)],
        compiler_params=_params("arbitrary"),
        name="in_proj",
    )(x, *consts)


def _mlstm_kernel(kf_ref, qtf_ref, vtf_ref, ccf_ref, brf_ref, crf_ref,
                  kb_ref, qtb_ref, vtb_ref, ccb_ref, brb_ref, crb_ref,
                  hf_ref, hb_ref, cn_ref, m_ref):
    @pl.when(pl.program_id(1) == 0)
    def _():
        cn_ref[...] = jnp.zeros_like(cn_ref)
        m_ref[...] = jnp.zeros_like(m_ref)

    r = lax.broadcasted_iota(jnp.int32, (CHUNK, CHUNK), 0)
    c = lax.broadcasted_iota(jnp.int32, (CHUNK, CHUNK), 1)
    masks = (r <= c, r >= c)
    ones_rows = jnp.where(lax.broadcasted_iota(jnp.int32, (M_DH, CHUNK), 0) == 0, 1.0, 0.0).astype(BF16)
    refs = ((kf_ref, qtf_ref, vtf_ref, ccf_ref, brf_ref, crf_ref, hf_ref),
            (kb_ref, qtb_ref, vtb_ref, ccb_ref, brb_ref, crb_ref, hb_ref))
    def load_jobs(step):
        jobs = []
        for dirn in range(2):
            k_ref, qt_ref, vt_ref, cc_ref, br_ref, cr_ref, h_ref = refs[dirn]
            ci = step if dirn == 0 else SCAN_CHUNKS - 1 - step
            sl = slice(ci * CHUNK, (ci + 1) * CHUNK)
            end = CHUNK - 1 if dirn == 0 else 0
            for hd in range(M_HEADS):
                idx = dirn * M_HEADS + hd
                hs = slice(hd * M_DH, (hd + 1) * M_DH)
                b_row = br_ref[idx:idx + 1, sl]
                jobs.append(dict(
                    idx=idx, out=(h_ref, sl, hs), mask=masks[dirn], k=k_ref[sl, hs], qt=qt_ref[hs, sl],
                    vt_aug=jnp.concatenate([vt_ref[hs, sl], ones_rows], axis=0), c_col=cc_ref[sl, idx:idx + 1],
                    b_row=b_row, c_row=cr_ref[idx:idx + 1, sl], b_end=b_row[:, end:end + 1]))
        return jobs

    def carry_free(jobs):
        for j in jobs:
            j["qk"] = _dot(j["k"], j["qt"])
            w_log = j["b_end"] - j["c_row"]
            j["w_max"] = jnp.max(w_log, axis=1, keepdims=True)
            vw = j["vt_aug"] * jnp.exp(w_log - j["w_max"]).astype(BF16)
            j["kv"] = _dot(vw, j["k"])

    def decay_weights(jobs):
        for j in jobs:
            m_prev = m_ref[j["idx"]][0:1, :]
            d = jnp.where(j["mask"], j["b_row"] - j["c_col"], -jnp.inf)
            a = j["b_row"] + m_prev
            m_t = jnp.maximum(a, jnp.max(d, axis=0, keepdims=True))
            j["m_t"] = m_t
            j["s"] = (j["qk"] * jnp.exp(d - m_t)).astype(BF16)
            j["q_carry"] = j["qt"] * jnp.exp(a - m_t).astype(BF16)
            j["a_end"] = j["b_end"] + m_prev

    def outputs_and_state(jobs):
        for j in jobs:
            cn_prev = cn_ref[j["idx"]]
            j["nd"] = _dot(jnp.concatenate([cn_prev.astype(BF16), j["vt_aug"]], axis=1),
                           jnp.concatenate([j["q_carry"], j["s"]], axis=0))
            m_new = jnp.maximum(j["a_end"], j["w_max"])
            cn_ref[j["idx"]] = jnp.exp(j["a_end"] - m_new) * cn_prev + jnp.exp(j["w_max"] - m_new) * j["kv"]
            m_ref[j["idx"]] = jnp.broadcast_to(m_new, m_ref.shape[1:])
        for j in jobs:
            nd = j["nd"]
            den = jnp.maximum(jnp.abs(nd[M_DH:M_DH + 1, :]), jnp.exp(-j["m_t"]))
            h_ref, sl, hs = j["out"]
            h_ref[sl, hs] = (nd[:M_DH, :] / den).T.astype(BF16)

    jobs = load_jobs(0)
    carry_free(jobs)
    for step in range(SCAN_CHUNKS):
        nxt = load_jobs(step + 1) if step + 1 < SCAN_CHUNKS else None
        for n in range(len(jobs)):
            decay_weights(jobs[n:n + 1])
            if nxt is not None:
                carry_free(nxt[n:n + 1])
        outputs_and_state(jobs)
        jobs = nxt


def _mlstm_scan(k, qt, vt, c_col, b_row, c_row, nb, seq):
    t = k.shape[0]
    bt = SCAN_CHUNKS * CHUNK
    n = seq // bt
    fwd = lambda b, i: (b * n + i, 0)
    bwd = lambda b, i: (b * n + n - 1 - i, 0)
    fwd_r = lambda b, i: (0, b * n + i)
    bwd_r = lambda b, i: (0, b * n + n - 1 - i)

    def specs(tok_map, fm_map):
        return [pl.BlockSpec((bt, M_W), tok_map)] + [pl.BlockSpec((M_W, bt), fm_map)] * 2 \
               + [pl.BlockSpec((bt, 2 * M_HEADS), tok_map)] + [pl.BlockSpec((SUBLANES, bt), fm_map)] * 2

    args = (k, qt, vt, c_col, b_row, c_row)
    return pl.pallas_call(
        _mlstm_kernel,
        grid=(nb, n),
        in_specs=specs(fwd, fwd_r) + specs(bwd, bwd_r),
        out_specs=[pl.BlockSpec((bt, M_W), fwd), pl.BlockSpec((bt, M_W), bwd)],
        out_shape=[jax.ShapeDtypeStruct((t, M_W), BF16)] * 2,
        scratch_shapes=[pltpu.VMEM((2 * M_HEADS, 2 * M_DH, M_DH), F32),
                        pltpu.VMEM((2 * M_HEADS, SUBLANES, LANES), F32)],
        compiler_params=_params("parallel", "arbitrary"),
        name="mlstm_scan",
    )(*args, *args)


def _mix_attn_kernel(x_ref, uc_ref, hf_ref, hb_ref, o_ref, cg_ref, cbeta_ref, wo1_ref, bo1_ref, g1_ref, b1_ref,
                     wq_ref, kt_ref, v_ref, wo2_ref, g2_ref, b2_ref, out_ref):
    head_cols = [slice(hd * X_DH, (hd + 1) * X_DH) for hd in range(X_HEADS)]

    def branches(j):
        sl = j["rows"]
        uc = jax.nn.silu(_layer_norm(uc_ref[sl, :].astype(F32), cg_ref[...], cbeta_ref[...]))
        hm = (hf_ref[sl, :].astype(F32) + hb_ref[sl, :].astype(F32)) * o_ref[sl, :].astype(F32)
        j["mixed"] = jnp.concatenate([uc.astype(BF16), hm.astype(BF16)], axis=1)

    def out_proj(j):
        j["mix"] = _dot(j["mixed"], wo1_ref[...]) + bo1_ref[...]

    def norm1(j):
        j["x1"] = _layer_norm(ALPHA * x_ref[j["rows"], :] + j["mix"], g1_ref[...], b1_ref[...])

    def q_proj(j):
        j["qx"] = _dot(j["x1"].astype(BF16), wq_ref[...]).astype(BF16)

    def scores(j):
        j["sc"] = [_dot(j["qx"][:, hs], kt_ref[0, hs, :]) * (X_DH ** -0.5) for hs in head_cols]

    def softmax(j):
        probs = []
        for sc in j["sc"]:
            e = jnp.exp(sc - jnp.max(sc, axis=-1, keepdims=True))
            probs.append((e / jnp.sum(e, axis=-1, keepdims=True)).astype(BF16))
        j["pr"] = probs

    def context(j):
        j["ctx"] = jnp.concatenate([_dot(pr, v_ref[0, :, hs]).astype(BF16) for pr, hs in zip(j["pr"], head_cols)],
                                   axis=1)

    def attn_out(j):
        j["att"] = _dot(j["ctx"], wo2_ref[...])

    def norm2(j):
        out_ref[j["rows"], :] = _layer_norm(ALPHA * j["x1"] + j["att"], g2_ref[...], b2_ref[...])

    groups = [dict(rows=slice(r, r + MIX_ROWS)) for r in range(0, MIX_TM, MIX_ROWS)]
    for stage in (branches, out_proj, norm1, q_proj, scores, softmax, context, attn_out, norm2):
        for j in groups:
            stage(j)


def _mix_attn(x, uc, hf, hb, o, kt, v, weights, nb, seq):
    t = x.shape[0]
    tm = MIX_TM
    n = seq // tm
    tok = lambda b, i: (b * n + i, 0)
    (cg, cbeta, wo1, bo1, g1, b1, wq, wo2, g2, b2) = weights
    cs = _const_spec
    return pl.pallas_call(
        _mix_attn_kernel,
        grid=(nb, n),
        in_specs=[pl.BlockSpec((tm, D_MODEL), tok)] + [pl.BlockSpec((tm, M_W), tok)] * 4
                 + [cs(cg.shape), cs(cbeta.shape), cs(wo1.shape), cs(bo1.shape), cs(g1.shape), cs(b1.shape), cs(wq.shape),
                    pl.BlockSpec((1, D_MODEL, N_MEM), lambda b, i: (b, 0, 0)),
                    pl.BlockSpec((1, N_MEM, D_MODEL), lambda b, i: (b, 0, 0)),
                    cs(wo2.shape), cs(g2.shape), cs(b2.shape)],
        out_specs=pl.BlockSpec((tm, D_MODEL), tok),
        out_shape=jax.ShapeDtypeStruct((t, D_MODEL), F32),
        compiler_params=_params("parallel", "parallel"),
        name="mix_attn",
    )(x, uc, hf, hb, o, cg, cbeta, wo1, bo1, g1, b1, wq, kt, v, wo2, g2, b2)


def _ffn_halo_kernel(x_ref, w_ref, b_ref, out_ref):
    out_ref[...] = _dot(x_ref[...].astype(BF16), w_ref[...]) + b_ref[...]


def _ffn_halo(rows, w_gate, b_gate):
    r = rows.shape[0]
    return pl.pallas_call(
        _ffn_halo_kernel,
        grid=(1,),
        in_specs=[_const_spec(rows.shape), _const_spec(w_gate.shape), _const_spec(b_gate.shape)],
        out_specs=_const_spec((r, D_FF)),
        out_shape=jax.ShapeDtypeStruct((r, D_FF), F32),
        compiler_params=_params("arbitrary"),
        name="ffn_halo",
    )(rows, w_gate, b_gate)


def _gelu_tanh_x2(x):
    c = float(np.float32(np.sqrt(2.0 / np.pi)))
    return x * (1.0 + jnp.tanh(x * (c + (c * 0.044715) * (x * x))))


def _conv_ffn_kernel(x_ref, halo_ref, wg_ref, bg_ref, wv_ref, bv_ref, cw_ref, cb_ref, wd_ref, bd_ref,
                     g3_ref, b3_ref, out_ref):
    i = pl.program_id(1)
    n = pl.num_programs(1)
    tm = FFN_TM
    halo = halo_ref[0]
    w = D_FF // FFN_SPLITS
    nrow = tm // FFN_ROWS
    row = lax.broadcasted_iota(jnp.int32, (SUBLANES, 1), 0)
    xs = [x_ref[g * FFN_ROWS:(g + 1) * FFN_ROWS, :] for g in range(nrow)]
    xbs = [x.astype(BF16) for x in xs]
    accs = [jnp.zeros((FFN_ROWS, D_MODEL), F32) + bd_ref[...] for _ in range(nrow)]
    for s in range(FFN_SPLITS):
        sl = slice(s * w, (s + 1) * w)
        gps = [_dot(xb, wg_ref[:, sl]) + bg_ref[:, sl] for xb in xbs]
        vals = [_dot(xb, wv_ref[:, sl]) + bv_ref[:, sl] for xb in xbs]
        acts = []
        for g in range(nrow):
            before = gps[g - 1][FFN_ROWS - 1:, :] if g > 0 else jnp.where(i > 0, halo[0:1, sl], 0.0)
            after = gps[g + 1][0:1, :] if g + 1 < nrow else jnp.where(i < n - 1, halo[1:2, sl], 0.0)
            g_prev = pltpu.roll(gps[g], 1, 0)
            g_prev = jnp.concatenate([jnp.where(row == 0, before, g_prev[0:SUBLANES]), g_prev[SUBLANES:]], axis=0)
            g_next = pltpu.roll(gps[g], FFN_ROWS - 1, 0)
            g_next = jnp.concatenate([g_next[:FFN_ROWS - SUBLANES],
                                      jnp.where(row == SUBLANES - 1, after, g_next[FFN_ROWS - SUBLANES:])], axis=0)
            gate = g_prev * cw_ref[0:1, sl] + gps[g] * cw_ref[1:2, sl] + g_next * cw_ref[2:3, sl] + cb_ref[:, sl]
            acts.append((_gelu_tanh_x2(gate) * vals[g]).astype(BF16))
        accs = [acc + _dot(act, wd_ref[sl, :]) for acc, act in zip(accs, acts)]
    for g in range(nrow):
        out_ref[g * FFN_ROWS:(g + 1) * FFN_ROWS, :] = _layer_norm(ALPHA * xs[g] + accs[g], g3_ref[...], b3_ref[...])


def _conv_ffn(x2, weights, nb, seq):
    t = x2.shape[0]
    tm = FFN_TM
    n = seq // tm
    (wg, bg, wv, bv, cw, cb, wd, bd, g3, b3) = weights
    xr = x2.reshape(nb, n, tm, D_MODEL)
    last = xr[:, :, tm - 1]
    first = xr[:, :, 0]
    before = jnp.concatenate([last[:, :1], last[:, :-1]], axis=1)
    after = jnp.concatenate([first[:, 1:], first[:, -1:]], axis=1)
    rows = jnp.stack([before, after], axis=2).reshape(nb * n * 2, D_MODEL)
    halo = _ffn_halo(rows, wg, bg).reshape(nb * n, 2, D_FF)

    tok = lambda b, i: (b * n + i, 0)
    cs = _const_spec
    return pl.pallas_call(
        _conv_ffn_kernel,
        grid=(nb, n),
        in_specs=[pl.BlockSpec((tm, D_MODEL), tok),
                  pl.BlockSpec((1, 2, D_FF), lambda b, i: (b * n + i, 0, 0)),
                  _resident_spec(wg.shape), cs(bg.shape), _resident_spec(wv.shape), cs(bv.shape), cs(cw.shape),
                  cs(cb.shape), _resident_spec(wd.shape), cs(bd.shape), cs(g3.shape), cs(b3.shape)],
        out_specs=pl.BlockSpec((tm, D_MODEL), tok),
        out_shape=jax.ShapeDtypeStruct((t, D_MODEL), F32),
        compiler_params=_params("parallel", "parallel"),
        name="conv_ffn",
    )(x2, halo, wg, bg, wv, bv, cw, cb, wd, bd, g3, b3)


def _pad_cols(w, n):
    return jnp.pad(w, ((0, 0), (0, n - w.shape[1])))


def _prep_layer(w_in, b_in, conv_w, conv_b, cln_g, cln_b, w_out, b_out, ln1_g, ln1_b, xq_w, xkv_w, xo_w,
                ln2_g, ln2_b, up_w, up_b, fconv_w, fconv_b, down_w, down_b, ln3_g, ln3_b):
    row = lambda a: a.reshape(1, -1)
    wg = w_in[:, N_MAIN:].reshape(D_MODEL, 4, M_HEADS)
    bg = b_in[N_MAIN:].reshape(4, M_HEADS)
    gate_order = (1, 3, 0, 2)
    w_gate = _pad_cols(jnp.concatenate([wg[:, j] for j in gate_order], axis=1), LANES).astype(BF16)
    b_gate = _pad_cols(row(jnp.concatenate([bg[j] for j in gate_order])), LANES)
    conv_in = (jnp.broadcast_to(conv_w[:, None, :], (CONV_K, SUBLANES, CONV_W)), row(conv_b))
    seg = lambda a, j: a[..., j * CONV_W:(j + 1) * CONV_W]
    tok_segs, fm_segs = (0, 1, 3, 5), (2, 4)
    in_w = (jnp.concatenate([seg(w_in, j) for j in tok_segs], axis=1).astype(BF16),
            row(jnp.concatenate([seg(b_in, j) for j in tok_segs])),
            jnp.concatenate([seg(w_in, j).T for j in fm_segs], axis=0).astype(BF16),
            jnp.concatenate([seg(b_in, j) for j in fm_segs]).reshape(-1, 1),
            w_gate, b_gate, conv_in)
    kv_w = (xkv_w[:, :D_MODEL].T.astype(BF16), xkv_w[:, D_MODEL:].astype(BF16))
    mix_w = (row(cln_g), row(cln_b), w_out.astype(BF16), row(b_out), row(ln1_g), row(ln1_b),
             xq_w.astype(BF16), xo_w.astype(BF16), row(ln2_g), row(ln2_b))
    ffn_w = (up_w[:, :D_FF].astype(BF16), row(up_b[:D_FF]), up_w[:, D_FF:].astype(BF16), row(up_b[D_FF:]),
             fconv_w, row(fconv_b), (0.5 * down_w).astype(BF16), row(down_b), row(ln3_g), row(ln3_b))
    return in_w, kv_w, mix_w, ffn_w


def _encoder_layer(x, mem, prepped):
    in_w, kv_w, mix_w, ffn_w = prepped
    nb, seq, _ = x.shape
    xf = x.reshape(nb * seq, D_MODEL)
    kt, v_mem = _kv_proj(mem, *kv_w)
    uc, k, o, qt, vt, c_col, b_row, c_row = _in_proj(xf, seq, *in_w)
    hf, hb = _mlstm_scan(k, qt, vt, c_col, b_row, c_row, nb, seq)
    x2 = _mix_attn(xf, uc, hf, hb, o, kt, v_mem, mix_w, nb, seq)
    x3 = _conv_ffn(x2, ffn_w, nb, seq)
    return x3.reshape(nb, seq, D_MODEL)


def kernel(x_prompt, x_sample, mem_prompt, mem_sample, w_in, b_in, conv_w, conv_b, cln_g, cln_b, w_out, b_out,
           ln1_g, ln1_b, xq_w, xkv_w, xo_w, ln2_g, ln2_b, up_w, up_b, fconv_w, fconv_b, down_w, down_b,
           ln3_g, ln3_b):
    layer_w = (w_in, b_in, conv_w, conv_b, cln_g, cln_b, w_out, b_out, ln1_g, ln1_b, xq_w, xkv_w, xo_w,
               ln2_g, ln2_b, up_w, up_b, fconv_w, fconv_b, down_w, down_b, ln3_g, ln3_b)
    y_prompt, y_sample = x_prompt, x_sample
    for l in range(DEPTH):
        prepped = _prep_layer(*[w[l] for w in layer_w])
        y_prompt = _encoder_layer(y_prompt, mem_prompt, prepped)
        y_sample = _encoder_layer(y_sample, mem_sample, prepped)
    return (y_prompt, y_sample)
```
